```python
import jax, jax.numpy as jnp
from jax import lax
import numpy as np

D_MODEL = 1024
BATCH = 8
SEQ = 4096
DEPTH = 4

EPS = 1e-6
PLE_DIM = 256
D_FF = 2816
SC_WIDTH = D_MODEL
SC_KERNEL = 3
SSM_INNER = 2 * D_MODEL
SSM_HEADDIM = 64
SSM_HEADS = SSM_INNER // SSM_HEADDIM
SSM_GROUPS = 4
SSM_STATE = 128
SSM_CONV = 4
SSM_CHUNK = 128
SSM_CONV_DIM = SSM_INNER + 2 * SSM_GROUPS * SSM_STATE
PROJ_SIZES = (SC_WIDTH, SC_WIDTH, SC_WIDTH,
              SSM_INNER, SSM_CONV_DIM, SSM_HEADS,
              D_MODEL, D_MODEL)
PROJ_DIM = sum(PROJ_SIZES)
PROJ_SPLITS = tuple(int(v) for v in np.cumsum(PROJ_SIZES)[:-1])

kernel_name = "hybrid_shortconv_ssd_macaron_block"


def rmsnorm(x, g):
    xf = x.astype(jnp.float32)
    y = xf * lax.rsqrt(jnp.mean(xf * xf, axis=-1, keepdims=True) + EPS)
    return y.astype(x.dtype) * g


def grouped_rmsnorm(x, g, groups):
    shp = x.shape
    xf = x.astype(jnp.float32).reshape(shp[:-1] + (groups, shp[-1] // groups))
    y = xf * lax.rsqrt(jnp.mean(xf * xf, axis=-1, keepdims=True) + EPS)
    return y.reshape(shp).astype(x.dtype) * g


def swiglu(x, wg, wu, wd):
    return (jax.nn.silu(x @ wg) * (x @ wu)) @ wd


def causal_depthwise_conv(x, w):
    k, c = w.shape
    return lax.conv_general_dilated(
        x, w.reshape(k, 1, c).astype(x.dtype), window_strides=(1,), padding=[(k - 1, 0)],
        dimension_numbers=("NWC", "WIO", "NWC"), feature_group_count=c)


def ssd_chunked(xdt, a, bm, cm):
    b, s, h, p = xdt.shape
    g, n = bm.shape[2], bm.shape[3]
    r = h // g
    nc, L = s // SSM_CHUNK, SSM_CHUNK
    dt_ = xdt.dtype
    X = xdt.reshape(b, nc, L, g, r, p)
    A = a.reshape(b, nc, L, g, r).astype(jnp.float32)
    Bc = bm.reshape(b, nc, L, g, n)
    Cc = cm.reshape(b, nc, L, g, n)
    a_cum = jnp.cumsum(A, axis=2)
    causal = jnp.tril(jnp.ones((L, L), dtype=bool))[None, None, :, :, None, None]
    diff = a_cum[:, :, :, None] - a_cum[:, :, None, :]
    decay = jnp.exp(jnp.where(causal, diff, -jnp.inf)).astype(dt_)
    cb = jnp.einsum("bclgn,bcsgn->bclsg", Cc, Bc)
    y_diag = jnp.einsum("bclsg,bclsgr,bcsgrp->bclgrp", cb, decay, X)
    decay_to_end = jnp.exp(a_cum[:, :, -1:] - a_cum).astype(dt_)
    states = jnp.einsum("bclgn,bclgr,bclgrp->bcgrpn", Bc, decay_to_end, X)
    chunk_decay = jnp.exp(a_cum[:, :, -1]).astype(dt_)

    def step(carry, inp):
        st, dec = inp
        return carry * dec[..., None, None] + st, carry

    init = jnp.zeros((b, g, r, p, n), dtype=states.dtype)
    _, prev = lax.scan(step, init, (jnp.swapaxes(states, 0, 1), jnp.swapaxes(chunk_decay, 0, 1)))
    prev = jnp.swapaxes(prev, 0, 1)
    y_off = jnp.einsum("bclgn,bcgrpn,bclgr->bclgrp", Cc, prev, jnp.exp(a_cum).astype(dt_))
    return (y_diag + y_off).reshape(b, s, h, p)


def hybrid_mixer(u, w_in, sc_conv_w, sc_w_out, m_conv_w, m_conv_b, m_dt_bias, m_A_log, m_D,
                 m_norm, m_w_out, w_o):
    b, s, _ = u.shape
    proj = u @ w_in
    sc_b, sc_c, sc_x, m_z, m_xbc, m_dt, gate_a, gate_m = jnp.split(proj, PROJ_SPLITS, axis=-1)
    y_a = (sc_b * causal_depthwise_conv(sc_c * sc_x, sc_conv_w)) @ sc_w_out
    xbc = jax.nn.silu(causal_depthwise_conv(m_xbc, m_conv_w) + m_conv_b)
    xs, bm, cm = jnp.split(xbc, (SSM_INNER, SSM_INNER + SSM_GROUPS * SSM_STATE), axis=-1)
    dt = jax.nn.softplus((m_dt + m_dt_bias).astype(jnp.float32))
    A = -jnp.exp(m_A_log.astype(jnp.float32))
    xh = xs.reshape(b, s, SSM_HEADS, SSM_HEADDIM)
    y = ssd_chunked(xh * dt.astype(xh.dtype)[..., None], A * dt,
                    bm.reshape(b, s, SSM_GROUPS, SSM_STATE), cm.reshape(b, s, SSM_GROUPS, SSM_STATE))
    y = (y + m_D[:, None] * xh).reshape(b, s, SSM_INNER)
    y_m = grouped_rmsnorm(y * jax.nn.silu(m_z), m_norm, SSM_GROUPS) @ m_w_out
    merged = jax.nn.sigmoid(gate_a) * y_a + jax.nn.sigmoid(gate_m) * y_m
    return merged @ w_o


def setup_inputs(seed: int = 0) -> dict:
    key = jax.random.key(seed)
    ks = iter(jax.random.split(key, 40))

    def nrm(shape, fan_in):
        return jax.random.normal(next(ks), shape, jnp.float32) * (fan_in ** -0.5)

    def gain(shape):
        return 1.0 + 0.05 * jax.random.normal(next(ks), shape, jnp.float32)

    dt0 = jnp.exp(jax.random.uniform(next(ks), (DEPTH, SSM_HEADS), jnp.float32)
                  * (np.log(0.1) - np.log(0.001)) + np.log(0.001))
    return {
        "x": jax.random.normal(next(ks), (BATCH, SEQ, D_MODEL), jnp.float32),
        "p": jax.random.normal(next(ks), (DEPTH, BATCH, SEQ, PLE_DIM), jnp.float32),
        "ffn1_norm": gain((DEPTH, D_MODEL)),
        "ffn1_wg": nrm((DEPTH, D_MODEL, D_FF), D_MODEL),
        "ffn1_wu": nrm((DEPTH, D_MODEL, D_FF), D_MODEL),
        "ffn1_wd": nrm((DEPTH, D_FF, D_MODEL), D_FF),
        "mix_norm": gain((DEPTH, D_MODEL)),
        "w_in": nrm((DEPTH, D_MODEL, PROJ_DIM), D_MODEL),
        "sc_conv_w": nrm((DEPTH, SC_KERNEL, SC_WIDTH), SC_KERNEL),
        "sc_w_out": nrm((DEPTH, SC_WIDTH, D_MODEL), SC_WIDTH),
        "m_conv_w": nrm((DEPTH, SSM_CONV, SSM_CONV_DIM), SSM_CONV),
        "m_conv_b": 0.02 * jax.random.normal(next(ks), (DEPTH, SSM_CONV_DIM), jnp.float32),
        "m_dt_bias": dt0 + jnp.log(-jnp.expm1(-dt0)),
        "m_A_log": jnp.log(jax.random.uniform(next(ks), (DEPTH, SSM_HEADS), jnp.float32, 1.0, 16.0)),
        "m_D": gain((DEPTH, SSM_HEADS)),
        "m_norm": gain((DEPTH, SSM_INNER)),
        "m_w_out": nrm((DEPTH, SSM_INNER, D_MODEL), SSM_INNER),
        "w_o": nrm((DEPTH, D_MODEL, D_MODEL), D_MODEL),
        "ffn2_norm": gain((DEPTH, D_MODEL)),
        "ffn2_wg": nrm((DEPTH, D_MODEL, D_FF), D_MODEL),
        "ffn2_wu": nrm((DEPTH, D_MODEL, D_FF), D_MODEL),
        "ffn2_wd": nrm((DEPTH, D_FF, D_MODEL), D_FF),
        "ple_norm": gain((DEPTH, D_MODEL)),
        "ple_w_gate": nrm((DEPTH, D_MODEL, D_MODEL), D_MODEL),
        "ple_w_proj": nrm((DEPTH, PLE_DIM, D_MODEL), PLE_DIM),
        "final_norm": gain((D_MODEL,)),
    }


def reference(x, p, ffn1_norm, ffn1_wg, ffn1_wu, ffn1_wd, mix_norm, w_in, sc_conv_w, sc_w_out,
              m_conv_w, m_conv_b, m_dt_bias, m_A_log, m_D, m_norm, m_w_out, w_o,
              ffn2_norm, ffn2_wg, ffn2_wu, ffn2_wd, ple_norm, ple_w_gate, ple_w_proj, final_norm):
    h = x
    for i in range(DEPTH):
        h = h + 0.5 * swiglu(rmsnorm(h, ffn1_norm[i]), ffn1_wg[i], ffn1_wu[i], ffn1_wd[i])
        h = h + hybrid_mixer(rmsnorm(h, mix_norm[i]), w_in[i], sc_conv_w[i], sc_w_out[i],
                             m_conv_w[i], m_conv_b[i], m_dt_bias[i], m_A_log[i], m_D[i],
                             m_norm[i], m_w_out[i], w_o[i])
        h = h + 0.5 * swiglu(rmsnorm(h, ffn2_norm[i]), ffn2_wg[i], ffn2_wu[i], ffn2_wd[i])
        gate = jax.nn.sigmoid(rmsnorm(h, ple_norm[i]) @ ple_w_gate[i])
        h = h + gate * (p[i] @ ple_w_proj[i])
    return rmsnorm(h, final_norm)
```

```python
import functools

import jax
import jax.numpy as jnp
from jax import lax
from jax.experimental import pallas as pl
from jax.experimental.pallas import tpu as pltpu

D_MODEL = 1024
D_FF = 2816
PLE_DIM = 256
SC_WIDTH = 1024
SC_KERNEL = 3
SSM_INNER = 2048
SSM_HEADDIM = 64
SSM_HEADS = 32
SSM_GROUPS = 4
SSM_STATE = 128
SSM_CONV = 4
SSM_CHUNK = 128
SSM_CONV_DIM = SSM_INNER + 2 * SSM_GROUPS * SSM_STATE
GROUP_WIDTH = SSM_INNER // SSM_GROUPS
HEADS_PER_GROUP = SSM_HEADS // SSM_GROUPS
EPS = 1e-6

LANES = 128
CARRY_ROWS = 8

COL_SC = 0
COL_Z = 3 * SC_WIDTH
COL_XBC = COL_Z + SSM_INNER
COL_GATE = COL_XBC + SSM_CONV_DIM
COL_DT = COL_GATE + 2 * D_MODEL
PROJ_COLS = COL_DT + LANES
DT_SRC = 3 * SC_WIDTH + SSM_INNER + SSM_CONV_DIM

FF_SPLITS = ((0, 1280), (1280, 2816))

FFN_TILE = 512
MIX_IN_TILE = 256
MIX_OUT_TILE = 256
VMEM_LIMIT = 56 * 1024 * 1024

bf16 = jnp.bfloat16
f32 = jnp.float32


def _rms(x, g):
    return x * lax.rsqrt(jnp.mean(x * x, axis=-1, keepdims=True) + EPS) * g


def _silu(x):
    return x * jax.nn.sigmoid(x)


def _dot(a, b):
    return jnp.dot(a, b, preferred_element_type=f32)


def _split2(x):
    hi = x.astype(bf16)
    lo = (x - hi.astype(f32)).astype(bf16)
    return hi, lo


def _split3(x):
    hi = x.astype(bf16)
    r = x - hi.astype(f32)
    mid = r.astype(bf16)
    lo = (r - mid.astype(f32)).astype(bf16)
    return hi, mid, lo


def _ffn_kernel(*refs, ple, final):
    if ple:
        (h_ref, g_ref, wg_ref, wu_ref, wd_ref, p_ref, pg_ref, pwg_ref, pwp_ref, fg_ref, o_ref) = refs
    else:
        (h_ref, g_ref, wg_ref, wu_ref, wd_ref, o_ref) = refs
    h = h_ref[...]
    xn = _rms(h, g_ref[...]).astype(bf16)
    acc = None
    for lo, hi in FF_SPLITS:
        gate = _dot(xn, wg_ref[:, lo:hi])
        up = _dot(xn, wu_ref[:, lo:hi])
        act = (_silu(gate) * up).astype(bf16)
        part = _dot(act, wd_ref[lo:hi, :])
        acc = part if acc is None else acc + part
    h = h + 0.5 * acc
    if ple:
        gate = jax.nn.sigmoid(_dot(_rms(h, pg_ref[...]).astype(bf16), pwg_ref[...]))
        h = h + gate * _dot(p_ref[...].astype(bf16), pwp_ref[...])
    if final:
        h = _rms(h, fg_ref[...])
    o_ref[...] = h


def _layer_spec(layer, rows, cols, single=True):
    kw = dict(pipeline_mode=pl.Buffered(1)) if single else {}
    return pl.BlockSpec((None, rows, cols), lambda *_: (layer, 0, 0), **kw)


def _ffn(h, layer, norm, wg, wu, wd, ple_args=None, final_norm=None):
    t = h.shape[0]
    tm = min(FFN_TILE, t)
    assert t % tm == 0
    tok = lambda cols: pl.BlockSpec((tm, cols), lambda i: (i, 0))
    in_specs = [tok(D_MODEL), _layer_spec(layer, 1, D_MODEL), _layer_spec(layer, D_MODEL, D_FF),
                _layer_spec(layer, D_MODEL, D_FF), _layer_spec(layer, D_FF, D_MODEL)]
    args = [h, norm, wg, wu, wd]
    ple = ple_args is not None
    if ple:
        p, pnorm, pwg, pwp = ple_args
        in_specs += [pl.BlockSpec((None, tm, PLE_DIM), lambda i: (layer, i, 0)),
                     _layer_spec(layer, 1, D_MODEL), _layer_spec(layer, D_MODEL, D_MODEL),
                     _layer_spec(layer, PLE_DIM, D_MODEL),
                     pl.BlockSpec((1, D_MODEL), lambda i: (0, 0))]
        args += [p, pnorm, pwg, pwp, final_norm]
    return pl.pallas_call(
        functools.partial(_ffn_kernel, ple=ple, final=ple and layer == norm.shape[0] - 1),
        grid=(t // tm,),
        in_specs=in_specs,
        out_specs=tok(D_MODEL),
        out_shape=jax.ShapeDtypeStruct((t, D_MODEL), f32),
        compiler_params=pltpu.CompilerParams(dimension_semantics=("arbitrary",),
                                             vmem_limit_bytes=VMEM_LIMIT),
        name=f"ffn_ple{int(ple)}",
    )(*args)


def _causal_conv(buf_ref, x, w_ref, taps, first):
    ts = x.shape[0]

    @pl.when(first)
    def _():
        buf_ref[0:CARRY_ROWS, :] = jnp.zeros((CARRY_ROWS, x.shape[1]), f32)

    buf_ref[CARRY_ROWS:CARRY_ROWS + ts, :] = x
    out = x * w_ref[taps - 1:taps, :]
    for k in range(taps - 1):
        off = CARRY_ROWS - (taps - 1) + k
        out = out + buf_ref[off:off + ts, :] * w_ref[k:k + 1, :]
    buf_ref[0:CARRY_ROWS, :] = buf_ref[ts:ts + CARRY_ROWS, :]
    return out


def _mix_in_kernel(h_ref, g_ref, w_ref, scw_ref, mcw_ref, mcb_ref, dtb_ref,
                   scp_ref, z_ref, xs_ref, bc_ref, gt_ref, dt_ref, scbuf, mbuf):
    first = pl.program_id(1) == 0
    xn = _rms(h_ref[...], g_ref[...]).astype(bf16)
    sc = _dot(xn, w_ref[:, COL_SC:COL_Z])
    cx = sc[:, SC_WIDTH:2 * SC_WIDTH] * sc[:, 2 * SC_WIDTH:]
    scp_ref[...] = (sc[:, :SC_WIDTH] * _causal_conv(scbuf, cx, scw_ref, SC_KERNEL, first)).astype(bf16)
    z_ref[...] = _dot(xn, w_ref[:, COL_Z:COL_XBC]).astype(bf16)
    xbc = _dot(xn, w_ref[:, COL_XBC:COL_GATE])
    xbc = _silu(_causal_conv(mbuf, xbc, mcw_ref, SSM_CONV, first) + mcb_ref[...])
    xs_ref[...] = xbc[:, :SSM_INNER].astype(bf16)
    bc_ref[...] = xbc[:, SSM_INNER:].astype(bf16)
    gt_ref[...] = jax.nn.sigmoid(_dot(xn, w_ref[:, COL_GATE:COL_DT])).astype(bf16)
    dt_ref[...] = jax.nn.softplus(_dot(xn, w_ref[:, COL_DT:PROJ_COLS]) + dtb_ref[...])


def _mix_in(h, layer, norm, w_in, sc_conv_w, m_conv_w, m_conv_b, dt_bias):
    b, s, _ = h.shape
    ts = min(MIX_IN_TILE, s)
    assert s % ts == 0
    tok = lambda cols: pl.BlockSpec((None, ts, cols), lambda bi, j: (bi, j, 0))
    widths = (SC_WIDTH, SSM_INNER, SSM_INNER, 2 * SSM_GROUPS * SSM_STATE, 2 * D_MODEL)
    return pl.pallas_call(
        _mix_in_kernel,
        grid=(b, s // ts),
        in_specs=[tok(D_MODEL), _layer_spec(layer, 1, D_MODEL), _layer_spec(layer, D_MODEL, PROJ_COLS),
                  _layer_spec(layer, SC_KERNEL, SC_WIDTH), _layer_spec(layer, SSM_CONV, SSM_CONV_DIM),
                  _layer_spec(layer, 1, SSM_CONV_DIM), _layer_spec(layer, 1, LANES)],
        out_specs=[tok(c) for c in widths] + [tok(LANES)],
        out_shape=[jax.ShapeDtypeStruct((b, s, c), bf16) for c in widths]
        + [jax.ShapeDtypeStruct((b, s, LANES), f32)],
        scratch_shapes=[pltpu.VMEM((ts + CARRY_ROWS, SC_WIDTH), f32),
                        pltpu.VMEM((ts + CARRY_ROWS, SSM_CONV_DIM), f32)],
        compiler_params=pltpu.CompilerParams(dimension_semantics=("arbitrary", "arbitrary"),
                                             vmem_limit_bytes=VMEM_LIMIT),
        name="mix_in",
    )(h, norm, w_in, sc_conv_w, m_conv_w, m_conv_b, dt_bias)


def _ssd_chunk(xs, bm, cm, dt, a_row, d_row, expand, st_ref):
    L = SSM_CHUNK
    a = dt * a_row
    li = lax.broadcasted_iota(jnp.int32, (L, L), 0)
    si = lax.broadcasted_iota(jnp.int32, (L, L), 1)
    causal = li >= si
    tril = causal.astype(bf16)
    acum = sum(_dot(tril, part) for part in _split3(a))
    acum_t = acum.T
    dt_t = dt.T
    a_last = acum[L - 1:L, :]
    w = jnp.exp(a_last - acum) * dt
    stacked = jnp.concatenate([w, jnp.exp(acum)], axis=0)
    expd = sum(_dot(part, expand) for part in _split2(stacked))
    w_exp, eac_exp = expd[:L], expd[L:]
    lane = lax.broadcasted_iota(jnp.int32, (L, LANES), 1)
    low_half = lane < SSM_HEADDIM
    ys = []
    for g in range(SSM_GROUPS):
        gs = slice(g * SSM_STATE, (g + 1) * SSM_STATE)
        cs = slice(g * GROUP_WIDTH, (g + 1) * GROUP_WIDTH)
        bg, cg = bm[:, gs], cm[:, gs]
        cb = lax.dot_general(cg, bg, (((1,), (1,)), ((), ())), preferred_element_type=f32)
        y_pairs = []
        for pr in range(HEADS_PER_GROUP // 2):
            ms = []
            for k in range(2):
                hd = g * HEADS_PER_GROUP + 2 * pr + k
                diff = acum[:, hd:hd + 1] - acum_t[hd:hd + 1, :]
                decay = jnp.exp(jnp.where(causal, diff, -jnp.inf))
                ms.append((cb * decay * dt_t[hd:hd + 1, :]).astype(bf16))
            c0 = (g * HEADS_PER_GROUP + 2 * pr) * SSM_HEADDIM
            xp = xs[:, c0:c0 + LANES]
            zero = jnp.zeros_like(xp)
            rhs = jnp.concatenate([jnp.where(low_half, xp, zero), jnp.where(low_half, zero, xp)], axis=0)
            y_pairs.append(_dot(jnp.concatenate(ms, axis=1), rhs))
        st = st_ref[:, cs]
        y_off = _dot(cg, st.astype(bf16)) * eac_exp[:, cs]
        xw = (xs[:, cs].astype(f32) * w_exp[:, cs]).astype(bf16)
        new = lax.dot_general(bg, xw, (((0,), (0,)), ((), ())), preferred_element_type=f32)
        st_ref[:, cs] = st * eac_exp[L - 1:L, cs] + new
        ys.append(jnp.concatenate(y_pairs, axis=1) + y_off)
    return jnp.concatenate(ys, axis=1) + d_row * xs.astype(f32)


def _mix_out_kernel(h_ref, scp_ref, z_ref, xs_ref, bc_ref, gt_ref, dt_ref,
                    alog_ref, dexp_ref, expand_ref, mnorm_ref, mwo_ref, scwo_ref, wo_ref,
                    o_ref, st_ref, y_ref):
    @pl.when(pl.program_id(1) == 0)
    def _():
        st_ref[...] = jnp.zeros(st_ref.shape, f32)

    a_row = -jnp.exp(alog_ref[...])
    n_chunks = h_ref.shape[0] // SSM_CHUNK
    nbc = SSM_GROUPS * SSM_STATE

    def chunk(c, carry):
        rows = pl.ds(pl.multiple_of(c * SSM_CHUNK, SSM_CHUNK), SSM_CHUNK)
        y_ref[rows, :] = _ssd_chunk(xs_ref[rows, :], bc_ref[rows, 0:nbc], bc_ref[rows, nbc:2 * nbc],
                                    dt_ref[rows, :], a_row, dexp_ref[...], expand_ref[...], st_ref)
        return carry

    lax.fori_loop(0, n_chunks, chunk, 0)

    z = z_ref[...].astype(f32)
    yz = y_ref[...] * _silu(z)
    parts = []
    for g in range(SSM_GROUPS):
        cs = slice(g * GROUP_WIDTH, (g + 1) * GROUP_WIDTH)
        blk = yz[:, cs]
        parts.append((_rms(blk, mnorm_ref[:, cs])).astype(bf16))
    y_m = _dot(jnp.concatenate(parts, axis=1), mwo_ref[...])
    y_a = _dot(scp_ref[...], scwo_ref[...])
    gates = gt_ref[...].astype(f32)
    merged = (gates[:, :D_MODEL] * y_a + gates[:, D_MODEL:] * y_m).astype(bf16)
    o_ref[...] = h_ref[...] + _dot(merged, wo_ref[...])


def _mix_out(h, layer, scp, z, xs, bc, gt, dt, a_log, d_exp, expand, m_norm, m_w_out, sc_w_out, w_o):
    b, s, _ = h.shape
    ts = min(MIX_OUT_TILE, s)
    assert s % ts == 0 and ts % SSM_CHUNK == 0
    tok = lambda cols: pl.BlockSpec((None, ts, cols), lambda bi, j: (bi, j, 0))
    return pl.pallas_call(
        _mix_out_kernel,
        grid=(b, s // ts),
        in_specs=[tok(D_MODEL), tok(SC_WIDTH), tok(SSM_INNER), tok(SSM_INNER),
                  tok(2 * SSM_GROUPS * SSM_STATE), tok(2 * D_MODEL), tok(LANES),
                  _layer_spec(layer, 1, LANES), _layer_spec(layer, 1, SSM_INNER),
                  pl.BlockSpec((LANES, SSM_INNER), lambda bi, j: (0, 0), pipeline_mode=pl.Buffered(1)),
                  _layer_spec(layer, 1, SSM_INNER), _layer_spec(layer, SSM_INNER, D_MODEL),
                  _layer_spec(layer, SC_WIDTH, D_MODEL), _layer_spec(layer, D_MODEL, D_MODEL)],
        out_specs=tok(D_MODEL),
        out_shape=jax.ShapeDtypeStruct((b, s, D_MODEL), f32),
        scratch_shapes=[pltpu.VMEM((SSM_STATE, SSM_INNER), f32), pltpu.VMEM((ts, SSM_INNER), f32)],
        compiler_params=pltpu.CompilerParams(dimension_semantics=("arbitrary", "arbitrary"),
                                             vmem_limit_bytes=VMEM_LIMIT),
        name="mix_out",
    )(h, scp, z, xs, bc, gt, dt, a_log, d_exp, expand, m_norm, m_w_out, sc_w_out, w_o)


def _row(x):
    return x.reshape(x.shape[0], 1, x.shape[1])


def _pad_lanes(x):
    return jnp.pad(x, ((0, 0), (0, LANES - x.shape[1])))


def kernel(x, p, ffn1_norm, ffn1_wg, ffn1_wu, ffn1_wd, mix_norm, w_in, sc_conv_w, sc_w_out, m_conv_w, m_conv_b, m_dt_bias, m_A_log, m_D, m_norm, m_w_out, w_o, ffn2_norm, ffn2_wg, ffn2_wu, ffn2_wd, ple_norm, ple_w_gate, ple_w_proj, final_norm):
    depth = w_in.shape[0]
    b, s, d = x.shape
    t = b * s
    cast = lambda w: w.astype(bf16)
    w_in_r = jnp.concatenate(
        [w_in[..., :DT_SRC], w_in[..., DT_SRC + SSM_HEADS:], w_in[..., DT_SRC:DT_SRC + SSM_HEADS],
         jnp.zeros(w_in.shape[:2] + (LANES - SSM_HEADS,), w_in.dtype)], axis=-1).astype(bf16)
    f1 = (_row(ffn1_norm), cast(ffn1_wg), cast(ffn1_wu), cast(ffn1_wd))
    f2 = (_row(ffn2_norm), cast(ffn2_wg), cast(ffn2_wu), cast(ffn2_wd))
    ple_w = (_row(ple_norm), cast(ple_w_gate), cast(ple_w_proj))
    mix_in_w = (_row(mix_norm), w_in_r, sc_conv_w, m_conv_w, _row(m_conv_b), _row(_pad_lanes(m_dt_bias)))
    expand = (jnp.arange(SSM_INNER)[None, :] // SSM_HEADDIM == jnp.arange(LANES)[:, None]).astype(bf16)
    mix_out_w = (_row(_pad_lanes(m_A_log)), _row(jnp.repeat(m_D, SSM_HEADDIM, axis=1)), expand,
                 _row(m_norm), cast(m_w_out), cast(sc_w_out), cast(w_o))
    p2 = p.reshape(depth, t, PLE_DIM)
    fnorm = final_norm.reshape(1, d)

    h = x.reshape(t, d)
    for i in range(depth):
        h = _ffn(h, i, *f1)
        h3 = h.reshape(b, s, d)
        mids = _mix_in(h3, i, *mix_in_w)
        h = _mix_out(h3, i, *mids, *mix_out_w).reshape(t, d)
        h = _ffn(h, i, *f2, ple_args=(p2,) + ple_w, final_norm=fnorm)
    return h.reshape(b, s, d)
```

```python
import functools

import jax
import jax.numpy as jnp
from jax import lax
from jax.experimental import pallas as pl
from jax.experimental.pallas import tpu as pltpu

D_MODEL = 1024
D_FF = 2816
PLE_DIM = 256
SC_WIDTH = 1024
SC_KERNEL = 3
SSM_INNER = 2048
SSM_HEADDIM = 64
SSM_HEADS = 32
SSM_GROUPS = 4
SSM_STATE = 128
SSM_CONV = 4
SSM_CHUNK = 128
SSM_CONV_DIM = SSM_INNER + 2 * SSM_GROUPS * SSM_STATE
GROUP_WIDTH = SSM_INNER // SSM_GROUPS
HEADS_PER_GROUP = SSM_HEADS // SSM_GROUPS
EPS = 1e-6

LANES = 128
SUBLANES = 8
ROW_BLOCK = 16
COL_CHUNK = 512

COL_SCB = 0
COL_CX = SC_WIDTH
COL_Z = 3 * SC_WIDTH
COL_XBC = COL_Z + SSM_INNER
COL_GATE = COL_XBC + SSM_CONV_DIM
COL_DT = COL_GATE + 2 * D_MODEL
PROJ_COLS = COL_DT + LANES
DT_SRC = 3 * SC_WIDTH + SSM_INNER + SSM_CONV_DIM

FF_SPLITS = ((0, 1280), (1280, 2816))

FFN_TILE = 512
MIX_IN_TILE = 256
MIX_OUT_TILE = 256
VMEM_LIMIT = 56 * 1024 * 1024

bf16 = jnp.bfloat16
f32 = jnp.float32


def _rms(x, g):
    return x * lax.rsqrt(jnp.mean(x * x, axis=-1, keepdims=True) + EPS) * g


def _sigmoid(x):
    return 0.5 * jnp.tanh(0.5 * x) + 0.5


def _silu(x):
    return x * _sigmoid(x)


def _dot(a, b):
    return jnp.dot(a, b, preferred_element_type=f32)


def _split2(x):
    hi = x.astype(bf16)
    lo = (x - hi.astype(f32)).astype(bf16)
    return hi, lo


def _split3(x):
    hi = x.astype(bf16)
    r = x - hi.astype(f32)
    mid = r.astype(bf16)
    lo = (r - mid.astype(f32)).astype(bf16)
    return hi, mid, lo


def _ffn_kernel(*refs, ple, final):
    if ple:
        (h_ref, g_ref, wg_ref, wu_ref, wd_ref, p_ref, pg_ref, pwg_ref, pwp_ref, fg_ref, o_ref) = refs
    else:
        (h_ref, g_ref, wg_ref, wu_ref, wd_ref, o_ref) = refs
    h = h_ref[...]
    xn = _rms(h, g_ref[...]).astype(bf16)
    acc = None
    for lo, hi in FF_SPLITS:
        gate = _dot(xn, wg_ref[:, lo:hi])
        up = _dot(xn, wu_ref[:, lo:hi])
        act = (_silu(gate) * up).astype(bf16)
        part = _dot(act, wd_ref[lo:hi, :])
        acc = part if acc is None else acc + part
    h = h + 0.5 * acc
    if ple:
        gate = _sigmoid(_dot(_rms(h, pg_ref[...]).astype(bf16), pwg_ref[...]))
        h = h + gate * _dot(p_ref[...].astype(bf16), pwp_ref[...])
    if final:
        h = _rms(h, fg_ref[...])
    o_ref[...] = h


def _layer_spec(layer, rows, cols):
    return pl.BlockSpec((None, rows, cols), lambda *_: (layer, 0, 0), pipeline_mode=pl.Buffered(1))


def _ffn(h, layer, norm, wg, wu, wd, ple_args=None, final_norm=None):
    t = h.shape[0]
    tm = min(FFN_TILE, t)
    assert t % tm == 0
    tok = lambda cols: pl.BlockSpec((tm, cols), lambda i: (i, 0))
    in_specs = [tok(D_MODEL), _layer_spec(layer, 1, D_MODEL), _layer_spec(layer, D_MODEL, D_FF),
                _layer_spec(layer, D_MODEL, D_FF), _layer_spec(layer, D_FF, D_MODEL)]
    args = [h, norm, wg, wu, wd]
    ple = ple_args is not None
    if ple:
        p, pnorm, pwg, pwp = ple_args
        in_specs += [pl.BlockSpec((None, tm, PLE_DIM), lambda i: (layer, i, 0)),
                     _layer_spec(layer, 1, D_MODEL), _layer_spec(layer, D_MODEL, D_MODEL),
                     _layer_spec(layer, PLE_DIM, D_MODEL),
                     pl.BlockSpec((1, D_MODEL), lambda i: (0, 0))]
        args += [p, pnorm, pwg, pwp, final_norm]
    return pl.pallas_call(
        functools.partial(_ffn_kernel, ple=ple, final=ple and layer == norm.shape[0] - 1),
        grid=(t // tm,),
        in_specs=in_specs,
        out_specs=tok(D_MODEL),
        out_shape=jax.ShapeDtypeStruct((t, D_MODEL), f32),
        compiler_params=pltpu.CompilerParams(dimension_semantics=("arbitrary",),
                                             vmem_limit_bytes=VMEM_LIMIT),
        name=f"ffn_ple{int(ple)}",
    )(*args)


TILES_PER_CHUNK = SSM_CHUNK // SUBLANES
HALO_TILES = SSM_CONV - 1
HALO_ROWS = HALO_TILES * SUBLANES
CHUNK_PITCH = HALO_ROWS + SSM_CHUNK
N_SC_CHUNKS = SC_WIDTH // COL_CHUNK
N_M_CHUNKS = SSM_CONV_DIM // COL_CHUNK


def _time_permutation(ts):
    q = jnp.arange(ts)
    t_of_q = (q // SSM_CHUNK) * SSM_CHUNK + (q % SUBLANES) * TILES_PER_CHUNK + (q % SSM_CHUNK) // SUBLANES
    return (t_of_q[:, None] == jnp.arange(ts)[None, :]).astype(bf16)


def _chunk_data(c):
    return HALO_ROWS + c * CHUNK_PITCH + HALO_ROWS


def _stage(buf_ref, x):
    n_chunks = x.shape[0] // SSM_CHUNK
    tail = SSM_CHUNK - HALO_ROWS
    first_sublane = lax.broadcasted_iota(jnp.int32, (SUBLANES, x.shape[1]), 0) == 0
    for c in range(n_chunks):
        buf_ref[_chunk_data(c):_chunk_data(c) + SSM_CHUNK, :] = x[c * SSM_CHUNK:(c + 1) * SSM_CHUNK, :]
    for c in range(n_chunks):
        prev0 = 0 if c == 0 else _chunk_data(c - 1) + tail
        cur0 = _chunk_data(c) + tail
        halo0 = _chunk_data(c) - HALO_ROWS
        for m in range(HALO_TILES):
            tile = slice(m * SUBLANES, (m + 1) * SUBLANES)
            cur = buf_ref[cur0 + tile.start:cur0 + tile.stop, :]
            prev = buf_ref[prev0 + tile.start:prev0 + tile.stop, :]
            buf_ref[halo0 + tile.start:halo0 + tile.stop, :] = jnp.where(
                first_sublane, pltpu.roll(prev, 1, axis=0), pltpu.roll(cur, 1, axis=0))


def _carry(buf_ref, ts):
    last = _chunk_data(ts // SSM_CHUNK - 1) + SSM_CHUNK - HALO_ROWS
    buf_ref[0:HALO_ROWS, :] = buf_ref[last:last + HALO_ROWS, :]


def _conv_block(buf_ref, w_ref, taps, r, wcols):
    base = _chunk_data(r // SSM_CHUNK) + r % SSM_CHUNK
    out = None
    for k in range(taps):
        off = base - (taps - 1 - k) * SUBLANES
        term = buf_ref[off:off + ROW_BLOCK, :] * w_ref[k * ROW_BLOCK:(k + 1) * ROW_BLOCK, wcols]
        out = term if out is None else out + term
    return out


def _mix_in_kernel(h_ref, g_ref, perm_ref, w_ref, scw_ref, mcw_ref, mcb_ref, dtb_ref,
                   scp_ref, z_ref, xs_ref, bc_ref, gt_ref, dt_ref, xnat_ref, xn_ref, *bufs):
    scbufs, mbufs = bufs[:N_SC_CHUNKS], bufs[N_SC_CHUNKS:]
    ts = h_ref.shape[0]
    row_blocks = range(0, ts, ROW_BLOCK)

    @pl.when(pl.program_id(1) == 0)
    def _():
        for buf in bufs:
            buf[0:HALO_ROWS, :] = jnp.zeros((HALO_ROWS, COL_CHUNK), f32)

    for r in row_blocks:
        rows = slice(r, r + ROW_BLOCK)
        xnat_ref[rows, :] = _rms(h_ref[rows, :], g_ref[...]).astype(bf16)
    xn_ref[...] = _dot(perm_ref[...], xnat_ref[...]).astype(bf16)

    def proj(c0, width=COL_CHUNK):
        return _dot(xn_ref[...], w_ref[:, c0:c0 + width])

    for i in range(N_SC_CHUNKS):
        halves = []
        for c0 in range(2 * i * COL_CHUNK, 2 * (i + 1) * COL_CHUNK, COL_CHUNK):
            cx = proj(COL_CX + c0)
            halves += [cx[:, 2 * t * LANES:(2 * t + 1) * LANES] * cx[:, (2 * t + 1) * LANES:(2 * t + 2) * LANES]
                       for t in range(COL_CHUNK // (2 * LANES))]
        _stage(scbufs[i], jnp.concatenate(halves, axis=1))

    def sc_job(i):
        cols = slice(i * COL_CHUNK, (i + 1) * COL_CHUNK)
        scb = proj(COL_SCB + i * COL_CHUNK)
        for r in row_blocks:
            conv = _conv_block(scbufs[i], scw_ref, SC_KERNEL, r, cols)
            scp_ref[r:r + ROW_BLOCK, cols] = (scb[r:r + ROW_BLOCK, :] * conv).astype(bf16)
        _carry(scbufs[i], ts)

    def m_start(i):
        _stage(mbufs[i], proj(COL_XBC + i * COL_CHUNK))

    def m_finish(i):
        c0 = i * COL_CHUNK
        cols = slice(c0, c0 + COL_CHUNK)
        if c0 < SSM_INNER:
            dst_ref, dcols = xs_ref, cols
        else:
            dst_ref, dcols = bc_ref, slice(c0 - SSM_INNER, c0 - SSM_INNER + COL_CHUNK)
        for r in row_blocks:
            conv = _conv_block(mbufs[i], mcw_ref, SSM_CONV, r, cols) + mcb_ref[:, cols]
            dst_ref[r:r + ROW_BLOCK, dcols] = _silu(conv).astype(bf16)
        _carry(mbufs[i], ts)

    def z_job(i):
        c0 = i * COL_CHUNK
        z_ref[:, c0:c0 + COL_CHUNK] = proj(COL_Z + c0).astype(bf16)

    def gate_job(i):
        c0 = i * COL_CHUNK
        gt_ref[:, c0:c0 + COL_CHUNK] = _sigmoid(proj(COL_GATE + c0)).astype(bf16)

    light = ([(z_job, i) for i in range(SSM_INNER // COL_CHUNK)]
             + [(gate_job, i) for i in range(2 * D_MODEL // COL_CHUNK)])
    for i in range(N_SC_CHUNKS):
        sc_job(i)
        job, j = light.pop(0)
        job(j)
    m_start(0)
    for i in range(N_M_CHUNKS):
        if i + 1 < N_M_CHUNKS:
            m_start(i + 1)
        job, j = light.pop(0)
        job(j)
        m_finish(i)
    assert not light
    dt_ref[...] = jax.nn.softplus(proj(COL_DT, LANES) + dtb_ref[...])


def _mix_in(h, layer, norm, w_in, sc_conv_w, m_conv_w, m_conv_b, dt_bias):
    b, s, _ = h.shape
    ts = min(MIX_IN_TILE, s)
    assert s % ts == 0 and ts % SSM_CHUNK == 0
    const = lambda shape: pl.BlockSpec(shape, lambda bi, j: (0, 0), pipeline_mode=pl.Buffered(1))
    tok = lambda cols: pl.BlockSpec((None, ts, cols), lambda bi, j: (bi, j, 0))
    widths = (SC_WIDTH, SSM_INNER, SSM_INNER, 2 * SSM_GROUPS * SSM_STATE, 2 * D_MODEL)
    return pl.pallas_call(
        _mix_in_kernel,
        grid=(b, s // ts),
        in_specs=[tok(D_MODEL), _layer_spec(layer, 1, D_MODEL), const((ts, ts)),
                  _layer_spec(layer, D_MODEL, PROJ_COLS),
                  _layer_spec(layer, SC_KERNEL * ROW_BLOCK, SC_WIDTH),
                  _layer_spec(layer, SSM_CONV * ROW_BLOCK, SSM_CONV_DIM),
                  _layer_spec(layer, ROW_BLOCK, SSM_CONV_DIM), _layer_spec(layer, 1, LANES)],
        out_specs=[tok(c) for c in widths] + [tok(LANES)],
        out_shape=[jax.ShapeDtypeStruct((b, s, c), bf16) for c in widths]
        + [jax.ShapeDtypeStruct((b, s, LANES), f32)],
        scratch_shapes=[pltpu.VMEM((ts, D_MODEL), bf16)] * 2
        + [pltpu.VMEM((HALO_ROWS + ts // SSM_CHUNK * CHUNK_PITCH, COL_CHUNK), f32)] * (N_SC_CHUNKS + N_M_CHUNKS),
        compiler_params=pltpu.CompilerParams(dimension_semantics=("arbitrary", "arbitrary"),
                                             vmem_limit_bytes=VMEM_LIMIT),
        name="mix_in",
    )(h, norm, _time_permutation(ts), w_in, sc_conv_w, m_conv_w, m_conv_b, dt_bias)


def _chunk_time(q):
    return (q & (SUBLANES - 1)) * TILES_PER_CHUNK + lax.shift_right_logical(q, SUBLANES.bit_length() - 1)


def _ssd_chunk(rows, xs_ref, bc_ref, dt_ref, a_row, dexp_ref, expand_ref, st_ref, y_ref):
    L = SSM_CHUNK
    nbc = SSM_GROUPS * SSM_STATE
    dt = dt_ref[rows, :]
    a = dt * a_row
    causal = (_chunk_time(lax.broadcasted_iota(jnp.int32, (L, L), 0))
              >= _chunk_time(lax.broadcasted_iota(jnp.int32, (L, L), 1)))
    tril = causal.astype(bf16)
    acum = _dot(jnp.concatenate([tril] * 3, axis=1), jnp.concatenate(_split3(a), axis=0))
    acum_t = acum.T
    dt_t = dt.T
    a_last = acum[L - 1:L, :]
    w = jnp.exp(a_last - acum) * dt
    stacked = jnp.concatenate([w, jnp.exp(acum)], axis=0)
    expd = _dot(jnp.concatenate(_split2(stacked), axis=1), expand_ref[...])
    lane = lax.broadcasted_iota(jnp.int32, (L, LANES), 1)
    low_half = lane < SSM_HEADDIM
    for g in range(SSM_GROUPS):
        cs = slice(g * GROUP_WIDTH, (g + 1) * GROUP_WIDTH)
        bg = bc_ref[rows, g * SSM_STATE:(g + 1) * SSM_STATE]
        cg = bc_ref[rows, nbc + g * SSM_STATE:nbc + (g + 1) * SSM_STATE]
        st = st_ref[:, cs]
        y_off = _dot(cg, st.astype(bf16)) * expd[L:, cs]
        xw = (xs_ref[rows, cs].astype(f32) * expd[:L, cs]).astype(bf16)
        new = lax.dot_general(bg, xw, (((0,), (0,)), ((), ())), preferred_element_type=f32)
        st_ref[:, cs] = st * expd[2 * L - 1:2 * L, cs] + new
        cb = lax.dot_general(cg, bg, (((1,), (1,)), ((), ())), preferred_element_type=f32)
        for pr in range(HEADS_PER_GROUP // 2):
            ms = []
            for k in range(2):
                hd = g * HEADS_PER_GROUP + 2 * pr + k
                diff = acum[:, hd:hd + 1] - acum_t[hd:hd + 1, :]
                decay = jnp.exp(jnp.where(causal, diff, -jnp.inf))
                ms.append((cb * decay * dt_t[hd:hd + 1, :]).astype(bf16))
            pc = slice(g * GROUP_WIDTH + pr * LANES, g * GROUP_WIDTH + (pr + 1) * LANES)
            xp = xs_ref[rows, pc]
            zero = jnp.zeros_like(xp)
            rhs = jnp.concatenate([jnp.where(low_half, xp, zero), jnp.where(low_half, zero, xp)], axis=0)
            y_ref[rows, pc] = (_dot(jnp.concatenate(ms, axis=1), rhs)
                               + y_off[:, pr * LANES:(pr + 1) * LANES] + dexp_ref[:, pc] * xp.astype(f32))


def _mix_out_kernel(h_ref, scp_ref, z_ref, xs_ref, bc_ref, gt_ref, dt_ref,
                    alog_ref, dexp_ref, expand_ref, unperm_ref, mnorm_ref, mwo_ref, scwo_ref, wo_ref,
                    o_ref, st_ref, y_ref, yn_ref, mg_ref, mgn_ref):
    ts = h_ref.shape[0]

    @pl.when(pl.program_id(1) == 0)
    def _():
        st_ref[...] = jnp.zeros(st_ref.shape, f32)

    a_row = -jnp.exp(alog_ref[...])
    for c in range(ts // SSM_CHUNK):
        _ssd_chunk(slice(c * SSM_CHUNK, (c + 1) * SSM_CHUNK), xs_ref, bc_ref, dt_ref, a_row,
                   dexp_ref, expand_ref, st_ref, y_ref)

    for g in range(SSM_GROUPS):
        cs = slice(g * GROUP_WIDTH, (g + 1) * GROUP_WIDTH)
        for r in range(0, ts, ROW_BLOCK):
            rows = slice(r, r + ROW_BLOCK)
            yz = y_ref[rows, cs] * _silu(z_ref[rows, cs].astype(f32))
            yn_ref[rows, cs] = _rms(yz, mnorm_ref[:, cs]).astype(bf16)
    for c0 in range(0, D_MODEL, COL_CHUNK):
        cols = slice(c0, c0 + COL_CHUNK)
        y_a = _dot(scp_ref[...], scwo_ref[:, cols])
        y_m = _dot(yn_ref[...], mwo_ref[:, cols])
        mg_ref[:, cols] = (gt_ref[:, cols].astype(f32) * y_a
                           + gt_ref[:, D_MODEL + c0:D_MODEL + c0 + COL_CHUNK].astype(f32) * y_m).astype(bf16)
    mgn_ref[...] = _dot(unperm_ref[...], mg_ref[...]).astype(bf16)
    for c0 in range(0, D_MODEL, COL_CHUNK):
        cols = slice(c0, c0 + COL_CHUNK)
        o_ref[:, cols] = h_ref[:, cols] + _dot(mgn_ref[...], wo_ref[:, cols])


def _mix_out(h, layer, scp, z, xs, bc, gt, dt, a_log, d_exp, expand, m_norm, m_w_out, sc_w_out, w_o):
    b, s, _ = h.shape
    ts = min(MIX_OUT_TILE, s)
    assert s % ts == 0 and ts % SSM_CHUNK == 0
    const = lambda shape: pl.BlockSpec(shape, lambda bi, j: (0, 0), pipeline_mode=pl.Buffered(1))
    tok = lambda cols: pl.BlockSpec((None, ts, cols), lambda bi, j: (bi, j, 0))
    return pl.pallas_call(
        _mix_out_kernel,
        grid=(b, s // ts),
        in_specs=[tok(D_MODEL), tok(SC_WIDTH), tok(SSM_INNER), tok(SSM_INNER),
                  tok(2 * SSM_GROUPS * SSM_STATE), tok(2 * D_MODEL), tok(LANES),
                  _layer_spec(layer, 1, LANES), _layer_spec(layer, 1, SSM_INNER),
                  const((2 * LANES, SSM_INNER)), const((ts, ts)),
                  _layer_spec(layer, 1, SSM_INNER), _layer_spec(layer, SSM_INNER, D_MODEL),
                  _layer_spec(layer, SC_WIDTH, D_MODEL), _layer_spec(layer, D_MODEL, D_MODEL)],
        out_specs=tok(D_MODEL),
        out_shape=jax.ShapeDtypeStruct((b, s, D_MODEL), f32),
        scratch_shapes=[pltpu.VMEM((SSM_STATE, SSM_INNER), f32), pltpu.VMEM((ts, SSM_INNER), f32),
                        pltpu.VMEM((ts, SSM_INNER), bf16), pltpu.VMEM((ts, D_MODEL), bf16),
                        pltpu.VMEM((ts, D_MODEL), bf16)],
        compiler_params=pltpu.CompilerParams(dimension_semantics=("arbitrary", "arbitrary"),
                                             vmem_limit_bytes=VMEM_LIMIT),
        name="mix_out",
    )(h, scp, z, xs, bc, gt, dt, a_log, d_exp, expand, _time_permutation(ts).T, m_norm, m_w_out, sc_w_out, w_o)


def _row(x):
    return x.reshape(x.shape[0], 1, x.shape[1])


def _pad_lanes(x):
    return jnp.pad(x, ((0, 0), (0, LANES - x.shape[1])))


def _arrange_w_in(w_in):
    depth, d, _ = w_in.shape
    tiles = SC_WIDTH // LANES
    w_c = w_in[..., SC_WIDTH:2 * SC_WIDTH].reshape(depth, d, tiles, 1, LANES)
    w_x = w_in[..., 2 * SC_WIDTH:3 * SC_WIDTH].reshape(depth, d, tiles, 1, LANES)
    w_cx = jnp.concatenate([w_c, w_x], axis=3).reshape(depth, d, 2 * SC_WIDTH)
    return jnp.concatenate(
        [w_in[..., :SC_WIDTH], w_cx, w_in[..., 3 * SC_WIDTH:DT_SRC], w_in[..., DT_SRC + SSM_HEADS:],
         w_in[..., DT_SRC:DT_SRC + SSM_HEADS], jnp.zeros((depth, d, LANES - SSM_HEADS), w_in.dtype)],
        axis=-1).astype(bf16)


def kernel(x, p, ffn1_norm, ffn1_wg, ffn1_wu, ffn1_wd, mix_norm, w_in, sc_conv_w, sc_w_out, m_conv_w, m_conv_b, m_dt_bias, m_A_log, m_D, m_norm, m_w_out, w_o, ffn2_norm, ffn2_wg, ffn2_wu, ffn2_wd, ple_norm, ple_w_gate, ple_w_proj, final_norm):
    depth = w_in.shape[0]
    b, s, d = x.shape
    t = b * s
    cast = lambda w: w.astype(bf16)
    f1 = (_row(ffn1_norm), cast(ffn1_wg), cast(ffn1_wu), cast(ffn1_wd))
    f2 = (_row(ffn2_norm), cast(ffn2_wg), cast(ffn2_wu), cast(ffn2_wd))
    ple_w = (_row(ple_norm), cast(ple_w_gate), cast(ple_w_proj))
    rep = lambda w: jnp.repeat(w, ROW_BLOCK, axis=1)
    mix_in_w = (_row(mix_norm), _arrange_w_in(w_in), rep(sc_conv_w), rep(m_conv_w), rep(_row(m_conv_b)),
                _row(_pad_lanes(m_dt_bias)))
    head_of_channel = jnp.arange(SSM_INNER)[None, :] // SSM_HEADDIM
    expand = (head_of_channel == jnp.arange(2 * LANES)[:, None] % LANES).astype(bf16)
    mix_out_w = (_row(_pad_lanes(m_A_log)), _row(jnp.repeat(m_D, SSM_HEADDIM, axis=1)), expand,
                 _row(m_norm), cast(m_w_out), cast(sc_w_out), cast(w_o))
    p2 = p.reshape(depth, t, PLE_DIM)
    fnorm = final_norm.reshape(1, d)

    h = x.reshape(t, d)
    for i in range(depth):
        h = _ffn(h, i, *f1)
        h3 = h.reshape(b, s, d)
        mids = _mix_in(h3, i, *mix_in_w)
        h = _mix_out(h3, i, *mids, *mix_out_w).reshape(t, d)
        h = _ffn(h, i, *f2, ple_args=(p2,) + ple_w, final_norm=fnorm)
    return h.reshape(b, s, d)
```

```python
import functools

import jax
import jax.numpy as jnp
from jax import lax
from jax.experimental import pallas as pl
from jax.experimental.pallas import tpu as pltpu

D_MODEL = 1024
D_FF = 2816
PLE_DIM = 256
SC_WIDTH = 1024
SC_KERNEL = 3
SSM_INNER = 2048
SSM_HEADDIM = 64
SSM_HEADS = 32
SSM_GROUPS = 4
SSM_STATE = 128
SSM_CONV = 4
SSM_CHUNK = 128
SSM_CONV_DIM = SSM_INNER + 2 * SSM_GROUPS * SSM_STATE
GROUP_WIDTH = SSM_INNER // SSM_GROUPS
HEADS_PER_GROUP = SSM_HEADS // SSM_GROUPS
EPS = 1e-6
LOG2E = 1.4426950408889634

LANES = 128
SUBLANES = 8
ROW_BLOCK = 16
COL_CHUNK = 512

CX_TILE = COL_CHUNK // 2
COL_XBC = SSM_INNER
ZX_COLS = SSM_INNER + SSM_CONV_DIM
DT_SRC = 3 * SC_WIDTH + ZX_COLS
TAIL_DT = 2 * D_MODEL
TAIL_COLS = TAIL_DT + LANES

FF_SPLITS = ((0, 1280), (1280, 2816))

FFN_TILE = 512
MIX_IN_TILE = 256
MIX_OUT_TILE = 512
VMEM_LIMIT = 56 * 1024 * 1024

bf16 = jnp.bfloat16
f32 = jnp.float32


def _rms(x, g):
    return x * lax.rsqrt(jnp.mean(x * x, axis=-1, keepdims=True) + EPS) * g


def _sigmoid(x):
    return 0.5 * jnp.tanh(0.5 * x) + 0.5


def _silu(x):
    return x * _sigmoid(x)


def _dot(a, b):
    return jnp.dot(a, b, preferred_element_type=f32)


def _split2(x):
    hi = x.astype(bf16)
    lo = (x - hi.astype(f32)).astype(bf16)
    return hi, lo


def _split3(x):
    hi = x.astype(bf16)
    r = x - hi.astype(f32)
    mid = r.astype(bf16)
    lo = (r - mid.astype(f32)).astype(bf16)
    return hi, mid, lo


def _ffn_kernel(*refs, ple, final):
    if ple:
        (h_ref, g_ref, wg_ref, wu_ref, wd_ref, p_ref, pg_ref, pwg_ref, pwp_ref, fg_ref, o_ref) = refs
    else:
        (h_ref, g_ref, wg_ref, wu_ref, wd_ref, o_ref) = refs
    h = h_ref[...]
    xn = _rms(h, g_ref[...]).astype(bf16)
    acc = None
    for lo, hi in FF_SPLITS:
        gate = _dot(xn, wg_ref[:, lo:hi])
        up = _dot(xn, wu_ref[:, lo:hi])
        act = (_silu(gate) * up).astype(bf16)
        part = _dot(act, wd_ref[lo:hi, :])
        acc = part if acc is None else acc + part
    h = h + 0.5 * acc
    if ple:
        gate = _sigmoid(_dot(_rms(h, pg_ref[...]).astype(bf16), pwg_ref[...]))
        h = h + gate * _dot(p_ref[...].astype(bf16), pwp_ref[...])
    if final:
        h = _rms(h, fg_ref[...])
    o_ref[...] = h


def _layer_spec(layer, rows, cols):
    return pl.BlockSpec((None, rows, cols), lambda *_: (layer, 0, 0), pipeline_mode=pl.Buffered(1))


def _ffn(h, layer, norm, wg, wu, wd, ple_args=None, final_norm=None):
    t = h.shape[0]
    tm = min(FFN_TILE, t)
    assert t % tm == 0
    tok = lambda cols: pl.BlockSpec((tm, cols), lambda i: (i, 0))
    in_specs = [tok(D_MODEL), _layer_spec(layer, 1, D_MODEL), _layer_spec(layer, D_MODEL, D_FF),
                _layer_spec(layer, D_MODEL, D_FF), _layer_spec(layer, D_FF, D_MODEL)]
    args = [h, norm, wg, wu, wd]
    ple = ple_args is not None
    if ple:
        p, pnorm, pwg, pwp = ple_args
        in_specs += [pl.BlockSpec((None, tm, PLE_DIM), lambda i: (layer, i, 0)),
                     _layer_spec(layer, 1, D_MODEL), _layer_spec(layer, D_MODEL, D_MODEL),
                     _layer_spec(layer, PLE_DIM, D_MODEL),
                     pl.BlockSpec((1, D_MODEL), lambda i: (0, 0))]
        args += [p, pnorm, pwg, pwp, final_norm]
    return pl.pallas_call(
        functools.partial(_ffn_kernel, ple=ple, final=ple and layer == norm.shape[0] - 1),
        grid=(t // tm,),
        in_specs=in_specs,
        out_specs=tok(D_MODEL),
        out_shape=jax.ShapeDtypeStruct((t, D_MODEL), f32),
        compiler_params=pltpu.CompilerParams(dimension_semantics=("arbitrary",),
                                             vmem_limit_bytes=VMEM_LIMIT),
        name=f"ffn_ple{int(ple)}",
    )(*args)


TILES_PER_CHUNK = SSM_CHUNK // SUBLANES
HALO_TILES = SSM_CONV - 1
HALO_ROWS = HALO_TILES * SUBLANES
CHUNK_PITCH = HALO_ROWS + SSM_CHUNK
N_SC_CHUNKS = SC_WIDTH // COL_CHUNK
N_M_CHUNKS = SSM_CONV_DIM // COL_CHUNK


def _time_permutation(ts):
    q = jnp.arange(ts)
    t_of_q = (q // SSM_CHUNK) * SSM_CHUNK + (q % SUBLANES) * TILES_PER_CHUNK + (q % SSM_CHUNK) // SUBLANES
    return (t_of_q[:, None] == jnp.arange(ts)[None, :]).astype(bf16)


def _chunk_data(c):
    return HALO_ROWS + c * CHUNK_PITCH + HALO_ROWS


def _stage(buf_ref, x):
    n_chunks = x.shape[0] // SSM_CHUNK
    tail = SSM_CHUNK - HALO_ROWS
    first_sublane = lax.broadcasted_iota(jnp.int32, (SUBLANES, x.shape[1]), 0) == 0
    for c in range(n_chunks):
        buf_ref[_chunk_data(c):_chunk_data(c) + SSM_CHUNK, :] = x[c * SSM_CHUNK:(c + 1) * SSM_CHUNK, :]
    for c in range(n_chunks):
        prev0 = 0 if c == 0 else _chunk_data(c - 1) + tail
        cur0 = _chunk_data(c) + tail
        halo0 = _chunk_data(c) - HALO_ROWS
        for m in range(HALO_TILES):
            tile = slice(m * SUBLANES, (m + 1) * SUBLANES)
            cur = buf_ref[cur0 + tile.start:cur0 + tile.stop, :]
            prev = buf_ref[prev0 + tile.start:prev0 + tile.stop, :]
            buf_ref[halo0 + tile.start:halo0 + tile.stop, :] = jnp.where(
                first_sublane, pltpu.roll(prev, 1, axis=0), pltpu.roll(cur, 1, axis=0))


def _carry(buf_ref, ts):
    last = _chunk_data(ts // SSM_CHUNK - 1) + SSM_CHUNK - HALO_ROWS
    buf_ref[0:HALO_ROWS, :] = buf_ref[last:last + HALO_ROWS, :]


def _conv_block(buf_ref, w_ref, taps, r, wcols):
    base = _chunk_data(r // SSM_CHUNK) + r % SSM_CHUNK
    out = None
    for k in range(taps):
        off = base - (taps - 1 - k) * SUBLANES
        term = buf_ref[off:off + ROW_BLOCK, :] * w_ref[k * ROW_BLOCK:(k + 1) * ROW_BLOCK, wcols]
        out = term if out is None else out + term
    return out


def _mix_in_kernel(h_ref, g_ref, perm_ref, wb_ref, wcx_ref, wzx_ref, wt_ref, scw_ref, mcw_ref, mcb_ref, dtb_ref,
                   scp_ref, z_ref, xs_ref, bc_ref, gt_ref, dt_ref, xnat_ref, xn_ref, *bufs):
    scbufs, mbufs = bufs[:N_SC_CHUNKS], bufs[N_SC_CHUNKS:]
    ts = h_ref.shape[0]
    row_blocks = range(0, ts, ROW_BLOCK)

    @pl.when(pl.program_id(1) == 0)
    def _():
        for buf in bufs:
            buf[0:HALO_ROWS, :] = jnp.zeros((HALO_ROWS, COL_CHUNK), f32)

    for r in row_blocks:
        rows = slice(r, r + ROW_BLOCK)
        xnat_ref[rows, :] = _rms(h_ref[rows, :], g_ref[...]).astype(bf16)
    xn_ref[...] = _dot(perm_ref[...], xnat_ref[...]).astype(bf16)

    def proj(w_ref, c0, width=COL_CHUNK):
        return _dot(xn_ref[...], w_ref[:, c0:c0 + width])

    for i in range(N_SC_CHUNKS):
        prods = []
        for c0 in range(2 * i * COL_CHUNK, 2 * (i + 1) * COL_CHUNK, COL_CHUNK):
            cx = proj(wcx_ref, c0)
            prods.append(cx[:, :CX_TILE] * cx[:, CX_TILE:])
        _stage(scbufs[i], jnp.concatenate(prods, axis=1))

    def sc_job(i):
        cols = slice(i * COL_CHUNK, (i + 1) * COL_CHUNK)
        scb = proj(wb_ref, i * COL_CHUNK)
        for r in row_blocks:
            conv = _conv_block(scbufs[i], scw_ref, SC_KERNEL, r, cols)
            scp_ref[r:r + ROW_BLOCK, cols] = (scb[r:r + ROW_BLOCK, :] * conv).astype(bf16)
        _carry(scbufs[i], ts)

    def m_start(i):
        _stage(mbufs[i], proj(wzx_ref, COL_XBC + i * COL_CHUNK))

    def m_finish(i):
        c0 = i * COL_CHUNK
        cols = slice(c0, c0 + COL_CHUNK)
        if c0 < SSM_INNER:
            dst_ref, dcols = xs_ref, cols
        else:
            dst_ref, dcols = bc_ref, slice(c0 - SSM_INNER, c0 - SSM_INNER + COL_CHUNK)
        for r in row_blocks:
            conv = _conv_block(mbufs[i], mcw_ref, SSM_CONV, r, cols) + mcb_ref[:, cols]
            dst_ref[r:r + ROW_BLOCK, dcols] = _silu(conv).astype(bf16)
        _carry(mbufs[i], ts)

    def z_job(i):
        c0 = i * COL_CHUNK
        z_ref[:, c0:c0 + COL_CHUNK] = proj(wzx_ref, c0).astype(bf16)

    def gate_job(i):
        c0 = i * COL_CHUNK
        gt_ref[:, c0:c0 + COL_CHUNK] = _sigmoid(proj(wt_ref, c0)).astype(bf16)

    light = ([(z_job, i) for i in range(SSM_INNER // COL_CHUNK)]
             + [(gate_job, i) for i in range(2 * D_MODEL // COL_CHUNK)])
    for i in range(N_SC_CHUNKS):
        sc_job(i)
        job, j = light.pop(0)
        job(j)
    m_start(0)
    for i in range(N_M_CHUNKS):
        if i + 1 < N_M_CHUNKS:
            m_start(i + 1)
        job, j = light.pop(0)
        job(j)
        m_finish(i)
    assert not light
    dt_ref[...] = jax.nn.softplus(proj(wt_ref, TAIL_DT, LANES) + dtb_ref[...])


def _mix_in(h, layer, norm, w_scb, w_cx, w_zx, w_tail, sc_conv_w, m_conv_w, m_conv_b, dt_bias):
    b, s, _ = h.shape
    ts = min(MIX_IN_TILE, s)
    assert s % ts == 0 and ts % SSM_CHUNK == 0
    const = lambda shape: pl.BlockSpec(shape, lambda bi, j: (0, 0), pipeline_mode=pl.Buffered(1))
    tok = lambda cols: pl.BlockSpec((None, ts, cols), lambda bi, j: (bi, j, 0))
    widths = (SC_WIDTH, SSM_INNER, SSM_INNER, 2 * SSM_GROUPS * SSM_STATE, 2 * D_MODEL)
    return pl.pallas_call(
        _mix_in_kernel,
        grid=(b, s // ts),
        in_specs=[tok(D_MODEL), _layer_spec(layer, 1, D_MODEL), const((ts, ts)),
                  _layer_spec(layer, D_MODEL, SC_WIDTH), _layer_spec(layer, D_MODEL, 2 * SC_WIDTH),
                  _layer_spec(layer, D_MODEL, ZX_COLS), _layer_spec(layer, D_MODEL, TAIL_COLS),
                  _layer_spec(layer, SC_KERNEL * ROW_BLOCK, SC_WIDTH),
                  _layer_spec(layer, SSM_CONV * ROW_BLOCK, SSM_CONV_DIM),
                  _layer_spec(layer, ROW_BLOCK, SSM_CONV_DIM), _layer_spec(layer, 1, LANES)],
        out_specs=[tok(c) for c in widths] + [tok(LANES)],
        out_shape=[jax.ShapeDtypeStruct((b, s, c), bf16) for c in widths]
        + [jax.ShapeDtypeStruct((b, s, LANES), f32)],
        scratch_shapes=[pltpu.VMEM((ts, D_MODEL), bf16)] * 2
        + [pltpu.VMEM((HALO_ROWS + ts // SSM_CHUNK * CHUNK_PITCH, COL_CHUNK), f32)] * (N_SC_CHUNKS + N_M_CHUNKS),
        compiler_params=pltpu.CompilerParams(dimension_semantics=("arbitrary", "arbitrary"),
                                             vmem_limit_bytes=VMEM_LIMIT),
        name="mix_in",
    )(h, norm, _time_permutation(ts), w_scb, w_cx, w_zx, w_tail, sc_conv_w, m_conv_w, m_conv_b, dt_bias)


def _chunk_time(q):
    return (q & (SUBLANES - 1)) * TILES_PER_CHUNK + lax.shift_right_logical(q, SUBLANES.bit_length() - 1)


def _ssd_chunk(rows, xs_ref, bc_ref, dt_ref, a_row, dexp_ref, expand_ref, st_ref, y_ref):
    L = SSM_CHUNK
    nbc = SSM_GROUPS * SSM_STATE
    dt = dt_ref[rows, :]
    a = dt * a_row
    causal = (_chunk_time(lax.broadcasted_iota(jnp.int32, (L, L), 0))
              >= _chunk_time(lax.broadcasted_iota(jnp.int32, (L, L), 1)))
    tril = causal.astype(bf16)
    acum = _dot(jnp.concatenate([tril] * 3, axis=1), jnp.concatenate(_split3(a), axis=0))
    acum2 = acum * LOG2E
    acum2_t = acum2.T
    dt_t = dt.T
    a_last = acum[L - 1:L, :]
    w = jnp.exp(a_last - acum) * dt
    stacked = jnp.concatenate([w, jnp.exp(acum)], axis=0)
    expd = _dot(jnp.concatenate(_split2(stacked), axis=1), expand_ref[...])
    lane = lax.broadcasted_iota(jnp.int32, (L, LANES), 1)
    low_half = lane < SSM_HEADDIM
    for g in range(SSM_GROUPS):
        cs = slice(g * GROUP_WIDTH, (g + 1) * GROUP_WIDTH)
        bg = bc_ref[rows, g * SSM_STATE:(g + 1) * SSM_STATE]
        cg = bc_ref[rows, nbc + g * SSM_STATE:nbc + (g + 1) * SSM_STATE]
        st = st_ref[:, cs]
        y_off = _dot(cg, st.astype(bf16)) * expd[L:, cs]
        xw = (xs_ref[rows, cs].astype(f32) * expd[:L, cs]).astype(bf16)
        new = lax.dot_general(bg, xw, (((0,), (0,)), ((), ())), preferred_element_type=f32)
        st_ref[:, cs] = st * expd[2 * L - 1:2 * L, cs] + new
        cb = lax.dot_general(cg, bg, (((1,), (1,)), ((), ())), preferred_element_type=f32)
        for pr in range(HEADS_PER_GROUP // 2):
            ms = []
            for k in range(2):
                hd = g * HEADS_PER_GROUP + 2 * pr + k
                diff = acum2[:, hd:hd + 1] - acum2_t[hd:hd + 1, :]
                decay = jnp.exp2(jnp.where(causal, diff, -jnp.inf))
                ms.append((cb * decay * dt_t[hd:hd + 1, :]).astype(bf16))
            pc = slice(g * GROUP_WIDTH + pr * LANES, g * GROUP_WIDTH + (pr + 1) * LANES)
            xp = xs_ref[rows, pc]
            zero = jnp.zeros_like(xp)
            rhs = jnp.concatenate([jnp.where(low_half, xp, zero), jnp.where(low_half, zero, xp)], axis=0)
            y_ref[rows, pc] = (_dot(jnp.concatenate(ms, axis=1), rhs)
                               + y_off[:, pr * LANES:(pr + 1) * LANES] + dexp_ref[:, pc] * xp.astype(f32))


def _mix_out_kernel(h_ref, scp_ref, z_ref, xs_ref, bc_ref, gt_ref, dt_ref,
                    alog_ref, dexp_ref, expand_ref, unperm_ref, mnorm_ref, mwo_ref, scwo_ref, wo_ref,
                    o_ref, st_ref, y_ref, yn_ref, mg_ref, mgn_ref):
    ts = h_ref.shape[0]

    @pl.when(pl.program_id(1) == 0)
    def _():
        st_ref[...] = jnp.zeros(st_ref.shape, f32)

    a_row = -jnp.exp(alog_ref[...])
    for c in range(ts // SSM_CHUNK):
        _ssd_chunk(slice(c * SSM_CHUNK, (c + 1) * SSM_CHUNK), xs_ref, bc_ref, dt_ref, a_row,
                   dexp_ref, expand_ref, st_ref, y_ref)

    for g in range(SSM_GROUPS):
        cs = slice(g * GROUP_WIDTH, (g + 1) * GROUP_WIDTH)
        for r in range(0, ts, ROW_BLOCK):
            rows = slice(r, r + ROW_BLOCK)
            yz = y_ref[rows, cs] * _silu(z_ref[rows, cs].astype(f32))
            yn_ref[rows, cs] = _rms(yz, mnorm_ref[:, cs]).astype(bf16)
    for c0 in range(0, D_MODEL, COL_CHUNK):
        cols = slice(c0, c0 + COL_CHUNK)
        y_a = _dot(scp_ref[...], scwo_ref[:, cols])
        y_m = _dot(yn_ref[...], mwo_ref[:, cols])
        mg_ref[:, cols] = (gt_ref[:, cols].astype(f32) * y_a
                           + gt_ref[:, D_MODEL + c0:D_MODEL + c0 + COL_CHUNK].astype(f32) * y_m).astype(bf16)
    mgn_ref[...] = _dot(unperm_ref[...], mg_ref[...]).astype(bf16)
    for c0 in range(0, D_MODEL, COL_CHUNK):
        cols = slice(c0, c0 + COL_CHUNK)
        o_ref[:, cols] = h_ref[:, cols] + _dot(mgn_ref[...], wo_ref[:, cols])


def _mix_out(h, layer, scp, z, xs, bc, gt, dt, a_log, d_exp, expand, m_norm, m_w_out, sc_w_out, w_o):
    b, s, _ = h.shape
    ts = min(MIX_OUT_TILE, s)
    assert s % ts == 0 and ts % SSM_CHUNK == 0
    const = lambda shape: pl.BlockSpec(shape, lambda bi, j: (0, 0), pipeline_mode=pl.Buffered(1))
    tok = lambda cols: pl.BlockSpec((None, ts, cols), lambda bi, j: (bi, j, 0))
    return pl.pallas_call(
        _mix_out_kernel,
        grid=(b, s // ts),
        in_specs=[tok(D_MODEL), tok(SC_WIDTH), tok(SSM_INNER), tok(SSM_INNER),
                  tok(2 * SSM_GROUPS * SSM_STATE), tok(2 * D_MODEL), tok(LANES),
                  _layer_spec(layer, 1, LANES), _layer_spec(layer, 1, SSM_INNER),
                  const((2 * LANES, SSM_INNER)), const((ts, ts)),
                  _layer_spec(layer, 1, SSM_INNER), _layer_spec(layer, SSM_INNER, D_MODEL),
                  _layer_spec(layer, SC_WIDTH, D_MODEL), _layer_spec(layer, D_MODEL, D_MODEL)],
        out_specs=tok(D_MODEL),
        out_shape=jax.ShapeDtypeStruct((b, s, D_MODEL), f32),
        scratch_shapes=[pltpu.VMEM((SSM_STATE, SSM_INNER), f32), pltpu.VMEM((ts, SSM_INNER), f32),
                        pltpu.VMEM((ts, SSM_INNER), bf16), pltpu.VMEM((ts, D_MODEL), bf16),
                        pltpu.VMEM((ts, D_MODEL), bf16)],
        compiler_params=pltpu.CompilerParams(dimension_semantics=("arbitrary", "arbitrary"),
                                             vmem_limit_bytes=VMEM_LIMIT),
        name="mix_out",
    )(h, scp, z, xs, bc, gt, dt, a_log, d_exp, expand, _time_permutation(ts).T, m_norm, m_w_out, sc_w_out, w_o)


def _row(x):
    return x.reshape(x.shape[0], 1, x.shape[1])


def _pad_lanes(x):
    return jnp.pad(x, ((0, 0), (0, LANES - x.shape[1])))


def _split_w_in(w_in):
    depth, d, _ = w_in.shape
    cx_tiles = []
    for c0 in range(0, SC_WIDTH, CX_TILE):
        cx_tiles += [w_in[..., SC_WIDTH + c0:SC_WIDTH + c0 + CX_TILE],
                     w_in[..., 2 * SC_WIDTH + c0:2 * SC_WIDTH + c0 + CX_TILE]]
    w_tail = jnp.concatenate(
        [w_in[..., DT_SRC + SSM_HEADS:], w_in[..., DT_SRC:DT_SRC + SSM_HEADS],
         jnp.zeros((depth, d, LANES - SSM_HEADS), w_in.dtype)], axis=-1)
    return (w_in[..., :SC_WIDTH].astype(bf16), jnp.concatenate(cx_tiles, axis=-1).astype(bf16),
            w_in[..., 3 * SC_WIDTH:DT_SRC].astype(bf16), w_tail.astype(bf16))


def kernel(x, p, ffn1_norm, ffn1_wg, ffn1_wu, ffn1_wd, mix_norm, w_in, sc_conv_w, sc_w_out, m_conv_w, m_conv_b, m_dt_bias, m_A_log, m_D, m_norm, m_w_out, w_o, ffn2_norm, ffn2_wg, ffn2_wu, ffn2_wd, ple_norm, ple_w_gate, ple_w_proj, final_norm):
    depth = w_in.shape[0]
    b, s, d = x.shape
    t = b * s
    cast = lambda w: w.astype(bf16)
    f1 = (_row(ffn1_norm), cast(ffn1_wg), cast(ffn1_wu), cast(ffn1_wd))
    f2 = (_row(ffn2_norm), cast(ffn2_wg), cast(ffn2_wu), cast(ffn2_wd))
    ple_w = (_row(ple_norm), cast(ple_w_gate), cast(ple_w_proj))
    rep = lambda w: jnp.repeat(w, ROW_BLOCK, axis=1)
    mix_in_w = (_row(mix_norm), *_split_w_in(w_in), rep(sc_conv_w), rep(m_conv_w), rep(_row(m_conv_b)),
                _row(_pad_lanes(m_dt_bias)))
    head_of_channel = jnp.arange(SSM_INNER)[None, :] // SSM_HEADDIM
    expand = (head_of_channel == jnp.arange(2 * LANES)[:, None] % LANES).astype(bf16)
    mix_out_w = (_row(_pad_lanes(m_A_log)), _row(jnp.repeat(m_D, SSM_HEADDIM, axis=1)), expand,
                 _row(m_norm), cast(m_w_out), cast(sc_w_out), cast(w_o))
    p2 = p.reshape(depth, t, PLE_DIM)
    fnorm = final_norm.reshape(1, d)

    h = x.reshape(t, d)
    for i in range(depth):
        h = _ffn(h, i, *f1)
        h3 = h.reshape(b, s, d)
        mids = _mix_in(h3, i, *mix_in_w)
        h = _mix_out(h3, i, *mids, *mix_out_w).reshape(t, d)
        h = _ffn(h, i, *f2, ple_args=(p2,) + ple_w, final_norm=fnorm)
    return h.reshape(b, s, d)
```

```python
import functools

import jax
import jax.numpy as jnp
from jax import lax
from jax.experimental import pallas as pl
from jax.experimental.pallas import tpu as pltpu

D_MODEL = 1024
D_FF = 2816
PLE_DIM = 256
SC_WIDTH = 1024
SC_KERNEL = 3
SSM_INNER = 2048
SSM_HEADDIM = 64
SSM_HEADS = 32
SSM_GROUPS = 4
SSM_STATE = 128
SSM_CONV = 4
SSM_CHUNK = 128
SSM_CONV_DIM = SSM_INNER + 2 * SSM_GROUPS * SSM_STATE
GROUP_WIDTH = SSM_INNER // SSM_GROUPS
HEADS_PER_GROUP = SSM_HEADS // SSM_GROUPS
EPS = 1e-6
LOG2E = 1.4426950408889634

LANES = 128
SUBLANES = 8
ROW_BLOCK = 16
COL_CHUNK = 512

CX_TILE = COL_CHUNK // 2
COL_SCB = 0
COL_CX = SC_WIDTH
COL_Z = 3 * SC_WIDTH
COL_XBC = COL_Z + SSM_INNER
COL_GATE = COL_XBC + SSM_CONV_DIM
COL_DT = COL_GATE + 2 * D_MODEL
PROJ_COLS = COL_DT + LANES
DT_SRC = COL_GATE

MID_SCP = 0
MID_Z = MID_SCP + SC_WIDTH
MID_XS = MID_Z + SSM_INNER
MID_BC = MID_XS + SSM_INNER
MID_GT = MID_BC + 2 * SSM_GROUPS * SSM_STATE
MID_COLS = MID_GT + 2 * D_MODEL

PM_MCB = 0
PM_MCW = PM_MCB + ROW_BLOCK
PM_SCW = PM_MCW + SSM_CONV * ROW_BLOCK
PM_NORM = PM_SCW + SC_KERNEL * ROW_BLOCK
PM_DTB = PM_NORM + SUBLANES
PM_ROWS = PM_DTB + SUBLANES

SP_D, SP_NORM, SP_ALOG, SP_ROWS = 0, 1, 2, SUBLANES
WO_M, WO_SC, WO_O = 0, SSM_INNER, SSM_INNER + SC_WIDTH
WO_ROWS = WO_O + D_MODEL

FF_SPLITS = ((0, 1280), (1280, 2816))

FFN_TILE = 512
MIX_IN_TILE = 256
MIX_OUT_TILE = 512
VMEM_LIMIT = 56 * 1024 * 1024

bf16 = jnp.bfloat16
f32 = jnp.float32


def _rms(x, g):
    return x * lax.rsqrt(jnp.mean(x * x, axis=-1, keepdims=True) + EPS) * g


def _sigmoid(x):
    return 0.5 * jnp.tanh(0.5 * x) + 0.5


def _silu(x):
    return x * _sigmoid(x)


def _dot(a, b):
    return jnp.dot(a, b, preferred_element_type=f32)


def _split2(x):
    hi = x.astype(bf16)
    lo = (x - hi.astype(f32)).astype(bf16)
    return hi, lo


def _split3(x):
    hi = x.astype(bf16)
    r = x - hi.astype(f32)
    mid = r.astype(bf16)
    lo = (r - mid.astype(f32)).astype(bf16)
    return hi, mid, lo


def _ffn_kernel(*refs, ple, final):
    if ple:
        (h_ref, g_ref, wg_ref, wu_ref, wd_ref, p_ref, pg_ref, pwg_ref, pwp_ref, fg_ref, o_ref) = refs
    else:
        (h_ref, g_ref, wg_ref, wu_ref, wd_ref, o_ref) = refs
    h = h_ref[...]
    xn = _rms(h, g_ref[...]).astype(bf16)
    acc = None
    for lo, hi in FF_SPLITS:
        gate = _dot(xn, wg_ref[:, lo:hi])
        up = _dot(xn, wu_ref[:, lo:hi])
        act = (_silu(gate) * up).astype(bf16)
        part = _dot(act, wd_ref[lo:hi, :])
        acc = part if acc is None else acc + part
    h = h + 0.5 * acc
    if ple:
        gate = _sigmoid(_dot(_rms(h, pg_ref[...]).astype(bf16), pwg_ref[...]))
        h = h + gate * _dot(p_ref[...].astype(bf16), pwp_ref[...])
    if final:
        h = _rms(h, fg_ref[...])
    o_ref[...] = h


def _layer_spec(layer, rows, cols):
    return pl.BlockSpec((None, rows, cols), lambda *_: (layer, 0, 0), pipeline_mode=pl.Buffered(1))


def _ffn(h, layer, norm, wg, wu, wd, ple_args=None, final_norm=None):
    t = h.shape[0]
    tm = min(FFN_TILE, t)
    assert t % tm == 0
    tok = lambda cols: pl.BlockSpec((tm, cols), lambda i: (i, 0))
    in_specs = [tok(D_MODEL), _layer_spec(layer, 1, D_MODEL), _layer_spec(layer, D_MODEL, D_FF),
                _layer_spec(layer, D_MODEL, D_FF), _layer_spec(layer, D_FF, D_MODEL)]
    args = [h, norm, wg, wu, wd]
    ple = ple_args is not None
    if ple:
        p, pnorm, pwg, pwp = ple_args
        in_specs += [pl.BlockSpec((None, tm, PLE_DIM), lambda i: (layer, i, 0)),
                     _layer_spec(layer, 1, D_MODEL), _layer_spec(layer, D_MODEL, D_MODEL),
                     _layer_spec(layer, PLE_DIM, D_MODEL),
                     pl.BlockSpec((1, D_MODEL), lambda i: (0, 0))]
        args += [p, pnorm, pwg, pwp, final_norm]
    return pl.pallas_call(
        functools.partial(_ffn_kernel, ple=ple, final=ple and layer == norm.shape[0] - 1),
        grid=(t // tm,),
        in_specs=in_specs,
        out_specs=tok(D_MODEL),
        out_shape=jax.ShapeDtypeStruct((t, D_MODEL), f32),
        compiler_params=pltpu.CompilerParams(dimension_semantics=("arbitrary",),
                                             vmem_limit_bytes=VMEM_LIMIT),
        name=f"ffn_ple{int(ple)}",
    )(*args)


TILES_PER_CHUNK = SSM_CHUNK // SUBLANES
HALO_TILES = SSM_CONV - 1
HALO_ROWS = HALO_TILES * SUBLANES
CHUNK_PITCH = HALO_ROWS + SSM_CHUNK
N_SC_CHUNKS = SC_WIDTH // COL_CHUNK
N_M_CHUNKS = SSM_CONV_DIM // COL_CHUNK


def _time_of_row(q):
    in_chunk = q & (SSM_CHUNK - 1)
    return (q - in_chunk) + (in_chunk & (SUBLANES - 1)) * TILES_PER_CHUNK + lax.shift_right_logical(
        in_chunk, SUBLANES.bit_length() - 1)


def _time_permutation(ts, inverse=False):
    rows = lax.broadcasted_iota(jnp.int32, (ts, ts), 0)
    cols = lax.broadcasted_iota(jnp.int32, (ts, ts), 1)
    hit = (rows == _time_of_row(cols)) if inverse else (_time_of_row(rows) == cols)
    return hit.astype(bf16)


def _chunk_data(c):
    return HALO_ROWS + c * CHUNK_PITCH + HALO_ROWS


def _stage(buf_ref, x):
    n_chunks = x.shape[0] // SSM_CHUNK
    tail = SSM_CHUNK - HALO_ROWS
    first_sublane = lax.broadcasted_iota(jnp.int32, (SUBLANES, x.shape[1]), 0) == 0
    for c in range(n_chunks):
        buf_ref[_chunk_data(c):_chunk_data(c) + SSM_CHUNK, :] = x[c * SSM_CHUNK:(c + 1) * SSM_CHUNK, :]
    for c in range(n_chunks):
        prev0 = 0 if c == 0 else _chunk_data(c - 1) + tail
        cur0 = _chunk_data(c) + tail
        halo0 = _chunk_data(c) - HALO_ROWS
        for m in range(HALO_TILES):
            tile = slice(m * SUBLANES, (m + 1) * SUBLANES)
            cur = buf_ref[cur0 + tile.start:cur0 + tile.stop, :]
            prev = buf_ref[prev0 + tile.start:prev0 + tile.stop, :]
            buf_ref[halo0 + tile.start:halo0 + tile.stop, :] = jnp.where(
                first_sublane, pltpu.roll(prev, 1, axis=0), pltpu.roll(cur, 1, axis=0))


def _carry(buf_ref, ts):
    last = _chunk_data(ts // SSM_CHUNK - 1) + SSM_CHUNK - HALO_ROWS
    buf_ref[0:HALO_ROWS, :] = buf_ref[last:last + HALO_ROWS, :]


def _conv_block(buf_ref, w_ref, w0, taps, r, wcols):
    base = _chunk_data(r // SSM_CHUNK) + r % SSM_CHUNK
    out = None
    for k in range(taps):
        off = base - (taps - 1 - k) * SUBLANES
        term = buf_ref[off:off + ROW_BLOCK, :] * w_ref[w0 + k * ROW_BLOCK:w0 + (k + 1) * ROW_BLOCK, wcols]
        out = term if out is None else out + term
    return out


def _mix_in_kernel(h_ref, pm_ref, w_ref, mid_ref, dt_ref, perm_ref, xnat_ref, xn_ref, *bufs):
    scbufs, mbufs = bufs[:N_SC_CHUNKS], bufs[N_SC_CHUNKS:]
    ts = h_ref.shape[0]
    row_blocks = range(0, ts, ROW_BLOCK)

    @pl.when((pl.program_id(0) == 0) & (pl.program_id(1) == 0))
    def _():
        perm_ref[...] = _time_permutation(ts)

    @pl.when(pl.program_id(1) == 0)
    def _():
        for buf in bufs:
            buf[0:HALO_ROWS, :] = jnp.zeros((HALO_ROWS, COL_CHUNK), f32)

    gain = pm_ref[PM_NORM:PM_NORM + 1, 0:D_MODEL]
    for r in row_blocks:
        rows = slice(r, r + ROW_BLOCK)
        xnat_ref[rows, :] = _rms(h_ref[rows, :], gain).astype(bf16)
    xn_ref[...] = _dot(perm_ref[...], xnat_ref[...]).astype(bf16)

    def proj(c0, width=COL_CHUNK):
        return _dot(xn_ref[...], w_ref[:, c0:c0 + width])

    for i in range(N_SC_CHUNKS):
        prods = []
        for c0 in range(2 * i * COL_CHUNK, 2 * (i + 1) * COL_CHUNK, COL_CHUNK):
            cx = proj(COL_CX + c0)
            prods.append(cx[:, :CX_TILE] * cx[:, CX_TILE:])
        _stage(scbufs[i], jnp.concatenate(prods, axis=1))

    def sc_job(i):
        cols = slice(i * COL_CHUNK, (i + 1) * COL_CHUNK)
        scb = proj(COL_SCB + i * COL_CHUNK)
        for r in row_blocks:
            conv = _conv_block(scbufs[i], pm_ref, PM_SCW, SC_KERNEL, r, cols)
            mid_ref[r:r + ROW_BLOCK, MID_SCP + cols.start:MID_SCP + cols.stop] = (
                scb[r:r + ROW_BLOCK, :] * conv).astype(bf16)
        _carry(scbufs[i], ts)

    def m_start(i):
        _stage(mbufs[i], proj(COL_XBC + i * COL_CHUNK))

    def m_finish(i):
        cols = slice(i * COL_CHUNK, (i + 1) * COL_CHUNK)
        for r in row_blocks:
            conv = (_conv_block(mbufs[i], pm_ref, PM_MCW, SSM_CONV, r, cols)
                    + pm_ref[PM_MCB:PM_MCB + ROW_BLOCK, cols])
            mid_ref[r:r + ROW_BLOCK, MID_XS + cols.start:MID_XS + cols.stop] = _silu(conv).astype(bf16)
        _carry(mbufs[i], ts)

    def z_job(i):
        c0 = i * COL_CHUNK
        mid_ref[:, MID_Z + c0:MID_Z + c0 + COL_CHUNK] = proj(COL_Z + c0).astype(bf16)

    def gate_job(i):
        c0 = i * COL_CHUNK
        mid_ref[:, MID_GT + c0:MID_GT + c0 + COL_CHUNK] = _sigmoid(proj(COL_GATE + c0)).astype(bf16)

    light = ([(z_job, i) for i in range(SSM_INNER // COL_CHUNK)]
             + [(gate_job, i) for i in range(2 * D_MODEL // COL_CHUNK)])
    for i in range(N_SC_CHUNKS):
        sc_job(i)
        job, j = light.pop(0)
        job(j)
    m_start(0)
    for i in range(N_M_CHUNKS):
        if i + 1 < N_M_CHUNKS:
            m_start(i + 1)
        job, j = light.pop(0)
        job(j)
        m_finish(i)
    assert not light
    dt_ref[...] = jax.nn.softplus(proj(COL_DT, LANES) + pm_ref[PM_DTB:PM_DTB + 1, 0:LANES])


def _mix_in(h, layer, params, w_all):
    b, s, _ = h.shape
    ts = min(MIX_IN_TILE, s)
    assert s % ts == 0 and ts % SSM_CHUNK == 0
    tok = lambda cols: pl.BlockSpec((None, ts, cols), lambda bi, j: (bi, j, 0))
    return pl.pallas_call(
        _mix_in_kernel,
        grid=(b, s // ts),
        in_specs=[tok(D_MODEL), _layer_spec(layer, PM_ROWS, SSM_CONV_DIM), _layer_spec(layer, D_MODEL, PROJ_COLS)],
        out_specs=[tok(MID_COLS), tok(LANES)],
        out_shape=[jax.ShapeDtypeStruct((b, s, MID_COLS), bf16), jax.ShapeDtypeStruct((b, s, LANES), f32)],
        scratch_shapes=[pltpu.VMEM((ts, ts), bf16)] + [pltpu.VMEM((ts, D_MODEL), bf16)] * 2
        + [pltpu.VMEM((HALO_ROWS + ts // SSM_CHUNK * CHUNK_PITCH, COL_CHUNK), f32)] * (N_SC_CHUNKS + N_M_CHUNKS),
        compiler_params=pltpu.CompilerParams(dimension_semantics=("arbitrary", "arbitrary"),
                                             vmem_limit_bytes=VMEM_LIMIT),
        name="mix_in",
    )(h, params, w_all)


def _chunk_time(q):
    return (q & (SUBLANES - 1)) * TILES_PER_CHUNK + lax.shift_right_logical(q, SUBLANES.bit_length() - 1)


def _ssd_chunk(rows, mid_ref, dt_ref, a_row, d_row, expand_ref, st_ref, y_ref):
    L = SSM_CHUNK
    nbc = SSM_GROUPS * SSM_STATE
    dt = dt_ref[rows, :]
    a = dt * a_row
    causal = (_chunk_time(lax.broadcasted_iota(jnp.int32, (L, L), 0))
              >= _chunk_time(lax.broadcasted_iota(jnp.int32, (L, L), 1)))
    tril = causal.astype(bf16)
    acum = _dot(jnp.concatenate([tril] * 3, axis=1), jnp.concatenate(_split3(a), axis=0))
    acum2 = acum * LOG2E
    acum2_t = acum2.T
    dt_t = dt.T
    a_last = acum[L - 1:L, :]
    w = jnp.exp(a_last - acum) * dt
    stacked = jnp.concatenate([w, jnp.exp(acum)], axis=0)
    expd = _dot(jnp.concatenate(_split2(stacked), axis=1), expand_ref[...])
    lane = lax.broadcasted_iota(jnp.int32, (L, LANES), 1)
    low_half = lane < SSM_HEADDIM
    for g in range(SSM_GROUPS):
        cs = slice(g * GROUP_WIDTH, (g + 1) * GROUP_WIDTH)
        bg = mid_ref[rows, MID_BC + g * SSM_STATE:MID_BC + (g + 1) * SSM_STATE]
        cg = mid_ref[rows, MID_BC + nbc + g * SSM_STATE:MID_BC + nbc + (g + 1) * SSM_STATE]
        st = st_ref[:, cs]
        y_off = _dot(cg, st.astype(bf16)) * expd[L:, cs]
        xw = (mid_ref[rows, MID_XS + cs.start:MID_XS + cs.stop].astype(f32) * expd[:L, cs]).astype(bf16)
        new = lax.dot_general(bg, xw, (((0,), (0,)), ((), ())), preferred_element_type=f32)
        st_ref[:, cs] = st * expd[2 * L - 1:2 * L, cs] + new
        cb = lax.dot_general(cg, bg, (((1,), (1,)), ((), ())), preferred_element_type=f32)
        for pr in range(HEADS_PER_GROUP // 2):
            ms = []
            for k in range(2):
                hd = g * HEADS_PER_GROUP + 2 * pr + k
                diff = acum2[:, hd:hd + 1] - acum2_t[hd:hd + 1, :]
                decay = jnp.exp2(jnp.where(causal, diff, -jnp.inf))
                ms.append((cb * decay * dt_t[hd:hd + 1, :]).astype(bf16))
            pc = slice(g * GROUP_WIDTH + pr * LANES, g * GROUP_WIDTH + (pr + 1) * LANES)
            xp = mid_ref[rows, MID_XS + pc.start:MID_XS + pc.stop]
            zero = jnp.zeros_like(xp)
            rhs = jnp.concatenate([jnp.where(low_half, xp, zero), jnp.where(low_half, zero, xp)], axis=0)
            y_ref[rows, pc] = (_dot(jnp.concatenate(ms, axis=1), rhs)
                               + y_off[:, pr * LANES:(pr + 1) * LANES] + d_row[:, pc] * xp.astype(f32))


def _mix_out_kernel(h_ref, mid_ref, dt_ref, sp_ref, w_ref, o_ref,
                    expand_ref, unperm_ref, st_ref, y_ref, yn_ref, mg_ref, mgn_ref):
    ts = h_ref.shape[0]

    @pl.when((pl.program_id(0) == 0) & (pl.program_id(1) == 0))
    def _():
        head_of_row = lax.broadcasted_iota(jnp.int32, expand_ref.shape, 0) & (LANES - 1)
        head_of_channel = lax.shift_right_logical(lax.broadcasted_iota(jnp.int32, expand_ref.shape, 1),
                                                  SSM_HEADDIM.bit_length() - 1)
        expand_ref[...] = (head_of_row == head_of_channel).astype(bf16)
        unperm_ref[...] = _time_permutation(ts, inverse=True)

    @pl.when(pl.program_id(1) == 0)
    def _():
        st_ref[...] = jnp.zeros(st_ref.shape, f32)

    a_row = -jnp.exp(sp_ref[SP_ALOG:SP_ALOG + 1, 0:LANES])
    d_row = sp_ref[SP_D:SP_D + 1, :]
    for c in range(ts // SSM_CHUNK):
        _ssd_chunk(slice(c * SSM_CHUNK, (c + 1) * SSM_CHUNK), mid_ref, dt_ref, a_row, d_row,
                   expand_ref, st_ref, y_ref)

    for g in range(SSM_GROUPS):
        cs = slice(g * GROUP_WIDTH, (g + 1) * GROUP_WIDTH)
        for r in range(0, ts, ROW_BLOCK):
            rows = slice(r, r + ROW_BLOCK)
            yz = y_ref[rows, cs] * _silu(mid_ref[rows, MID_Z + cs.start:MID_Z + cs.stop].astype(f32))
            yn_ref[rows, cs] = _rms(yz, sp_ref[SP_NORM:SP_NORM + 1, cs]).astype(bf16)
    for c0 in range(0, D_MODEL, COL_CHUNK):
        cols = slice(c0, c0 + COL_CHUNK)
        y_a = _dot(mid_ref[:, MID_SCP:MID_SCP + SC_WIDTH], w_ref[WO_SC:WO_SC + SC_WIDTH, cols])
        y_m = _dot(yn_ref[...], w_ref[WO_M:WO_M + SSM_INNER, cols])
        gate_a = mid_ref[:, MID_GT + c0:MID_GT + c0 + COL_CHUNK].astype(f32)
        gate_m = mid_ref[:, MID_GT + D_MODEL + c0:MID_GT + D_MODEL + c0 + COL_CHUNK].astype(f32)
        mg_ref[:, cols] = (gate_a * y_a + gate_m * y_m).astype(bf16)
    mgn_ref[...] = _dot(unperm_ref[...], mg_ref[...]).astype(bf16)
    for c0 in range(0, D_MODEL, COL_CHUNK):
        cols = slice(c0, c0 + COL_CHUNK)
        o_ref[:, cols] = h_ref[:, cols] + _dot(mgn_ref[...], w_ref[WO_O:WO_O + D_MODEL, cols])


def _mix_out(h, layer, mid, dt, params, w_out):
    b, s, _ = h.shape
    ts = min(MIX_OUT_TILE, s)
    assert s % ts == 0 and ts % SSM_CHUNK == 0
    tok = lambda cols: pl.BlockSpec((None, ts, cols), lambda bi, j: (bi, j, 0))
    return pl.pallas_call(
        _mix_out_kernel,
        grid=(b, s // ts),
        in_specs=[tok(D_MODEL), tok(MID_COLS), tok(LANES),
                  _layer_spec(layer, SP_ROWS, SSM_INNER), _layer_spec(layer, WO_ROWS, D_MODEL)],
        out_specs=tok(D_MODEL),
        out_shape=jax.ShapeDtypeStruct((b, s, D_MODEL), f32),
        scratch_shapes=[pltpu.VMEM((2 * LANES, SSM_INNER), bf16), pltpu.VMEM((ts, ts), bf16),
                        pltpu.VMEM((SSM_STATE, SSM_INNER), f32), pltpu.VMEM((ts, SSM_INNER), f32),
                        pltpu.VMEM((ts, SSM_INNER), bf16), pltpu.VMEM((ts, D_MODEL), bf16),
                        pltpu.VMEM((ts, D_MODEL), bf16)],
        compiler_params=pltpu.CompilerParams(dimension_semantics=("arbitrary", "arbitrary"),
                                             vmem_limit_bytes=VMEM_LIMIT),
        name="mix_out",
    )(h, mid, dt, params, w_out)


def _row(x):
    return x.reshape(x.shape[0], 1, x.shape[1])


def _rows(x, n_rows, width):
    return jnp.pad(x, ((0, 0), (0, n_rows - x.shape[1]), (0, width - x.shape[2])))


def _arrange_w_in(w_in):
    w = w_in.astype(bf16)
    cx_tiles = []
    for c0 in range(0, SC_WIDTH, CX_TILE):
        cx_tiles += [w[..., SC_WIDTH + c0:SC_WIDTH + c0 + CX_TILE],
                     w[..., 2 * SC_WIDTH + c0:2 * SC_WIDTH + c0 + CX_TILE]]
    return jnp.concatenate(
        [w[..., :SC_WIDTH], *cx_tiles, w[..., 3 * SC_WIDTH:DT_SRC], w[..., DT_SRC + SSM_HEADS:],
         w[..., DT_SRC:DT_SRC + SSM_HEADS], jnp.zeros(w.shape[:2] + (LANES - SSM_HEADS,), bf16)], axis=-1)


def kernel(x, p, ffn1_norm, ffn1_wg, ffn1_wu, ffn1_wd, mix_norm, w_in, sc_conv_w, sc_w_out, m_conv_w, m_conv_b, m_dt_bias, m_A_log, m_D, m_norm, m_w_out, w_o, ffn2_norm, ffn2_wg, ffn2_wu, ffn2_wd, ple_norm, ple_w_gate, ple_w_proj, final_norm):
    depth = w_in.shape[0]
    b, s, d = x.shape
    t = b * s
    cast = lambda w: w.astype(bf16)
    f1 = (_row(ffn1_norm), cast(ffn1_wg), cast(ffn1_wu), cast(ffn1_wd))
    f2 = (_row(ffn2_norm), cast(ffn2_wg), cast(ffn2_wu), cast(ffn2_wd))
    ple_w = (_row(ple_norm), cast(ple_w_gate), cast(ple_w_proj))
    rep = lambda w: jnp.repeat(w, ROW_BLOCK, axis=1)
    wide = lambda w, n_rows: _rows(w, n_rows, SSM_CONV_DIM)
    mix_in_p = jnp.concatenate(
        [rep(_row(m_conv_b)), rep(m_conv_w), wide(rep(sc_conv_w), SC_KERNEL * ROW_BLOCK),
         wide(_row(mix_norm), SUBLANES), wide(_row(m_dt_bias), SUBLANES)], axis=1)
    mix_out_p = jnp.concatenate(
        [_row(jnp.repeat(m_D, SSM_HEADDIM, axis=1)), _row(m_norm),
         _rows(_row(m_A_log), SP_ROWS - 2, SSM_INNER)], axis=1)
    w_all = _arrange_w_in(w_in)
    w_out = jnp.concatenate([cast(m_w_out), cast(sc_w_out), cast(w_o)], axis=1)
    p2 = p.reshape(depth, t, PLE_DIM)
    fnorm = final_norm.reshape(1, d)

    h = x.reshape(t, d)
    for i in range(depth):
        h = _ffn(h, i, *f1)
        h3 = h.reshape(b, s, d)
        mid, dt = _mix_in(h3, i, mix_in_p, w_all)
        h = _mix_out(h3, i, mid, dt, mix_out_p, w_out).reshape(t, d)
        h = _ffn(h, i, *f2, ple_args=(p2,) + ple_w, final_norm=fnorm)
    return h.reshape(b, s, d)
```

```python
import functools

import jax
import jax.numpy as jnp
from jax import lax
from jax.experimental import pallas as pl
from jax.experimental.pallas import tpu as pltpu

D_MODEL = 1024
D_FF = 2816
PLE_DIM = 256
SC_WIDTH = 1024
SC_KERNEL = 3
SSM_INNER = 2048
SSM_HEADDIM = 64
SSM_HEADS = 32
SSM_GROUPS = 4
SSM_STATE = 128
SSM_CONV = 4
SSM_CHUNK = 128
SSM_CONV_DIM = SSM_INNER + 2 * SSM_GROUPS * SSM_STATE
GROUP_WIDTH = SSM_INNER // SSM_GROUPS
HEADS_PER_GROUP = SSM_HEADS // SSM_GROUPS
EPS = 1e-6
LOG2E = 1.4426950408889634

LANES = 128
SUBLANES = 8
ROW_BLOCK = 16
COL_CHUNK = 512

CX_TILE = COL_CHUNK // 2
COL_SCB = 0
COL_CX = SC_WIDTH
COL_Z = 3 * SC_WIDTH
COL_XBC = COL_Z + SSM_INNER
COL_GATE = COL_XBC + SSM_CONV_DIM
COL_DT = COL_GATE + 2 * D_MODEL
PROJ_COLS = COL_DT + LANES
DT_SRC = COL_GATE

MID_SCP = 0
MID_Z = MID_SCP + SC_WIDTH
MID_XS = MID_Z + SSM_INNER
MID_BC = MID_XS + SSM_INNER
MID_GT = MID_BC + 2 * SSM_GROUPS * SSM_STATE
MID_COLS = MID_GT + 2 * D_MODEL

PM_MCB = 0
PM_MCW = PM_MCB + ROW_BLOCK
PM_SCW = PM_MCW + SSM_CONV * ROW_BLOCK
PM_NORM = PM_SCW + SC_KERNEL * ROW_BLOCK
PM_DTB = PM_NORM + SUBLANES
PM_ROWS = PM_DTB + SUBLANES

SP_D, SP_NORM, SP_ALOG, SP_ROWS = 0, 1, 2, SUBLANES
WO_M, WO_SC, WO_O = 0, SSM_INNER, SSM_INNER + SC_WIDTH
WO_ROWS = WO_O + D_MODEL

FF_SPLITS = ((0, 1280), (1280, 2816))

FFN_TILE = 512
MIX_TILE = 256
VMEM_LIMIT = 56 * 1024 * 1024

bf16 = jnp.bfloat16
f32 = jnp.float32


def _rms(x, g):
    return x * lax.rsqrt(jnp.mean(x * x, axis=-1, keepdims=True) + EPS) * g


def _sigmoid(x):
    return 0.5 * jnp.tanh(0.5 * x) + 0.5


def _silu(x):
    return x * _sigmoid(x)


def _dot(a, b):
    return jnp.dot(a, b, preferred_element_type=f32)


def _split2(x):
    hi = x.astype(bf16)
    lo = (x - hi.astype(f32)).astype(bf16)
    return hi, lo


def _split3(x):
    hi = x.astype(bf16)
    r = x - hi.astype(f32)
    mid = r.astype(bf16)
    lo = (r - mid.astype(f32)).astype(bf16)
    return hi, mid, lo


def _ffn_kernel(*refs, ple, final):
    if ple:
        (h_ref, g_ref, wg_ref, wu_ref, wd_ref, p_ref, pg_ref, pwg_ref, pwp_ref, fg_ref, o_ref) = refs
    else:
        (h_ref, g_ref, wg_ref, wu_ref, wd_ref, o_ref) = refs
    h = h_ref[...]
    xn = _rms(h, g_ref[...]).astype(bf16)
    acc = None
    for lo, hi in FF_SPLITS:
        gate = _dot(xn, wg_ref[:, lo:hi])
        up = _dot(xn, wu_ref[:, lo:hi])
        act = (_silu(gate) * up).astype(bf16)
        part = _dot(act, wd_ref[lo:hi, :])
        acc = part if acc is None else acc + part
    h = h + 0.5 * acc
    if ple:
        gate = _sigmoid(_dot(_rms(h, pg_ref[...]).astype(bf16), pwg_ref[...]))
        h = h + gate * _dot(p_ref[...].astype(bf16), pwp_ref[...])
    if final:
        h = _rms(h, fg_ref[...])
    o_ref[...] = h


def _layer_spec(layer, rows, cols):
    return pl.BlockSpec((None, rows, cols), lambda *_: (layer, 0, 0), pipeline_mode=pl.Buffered(1))


def _ffn(h, layer, norm, wg, wu, wd, ple_args=None, final_norm=None):
    t = h.shape[0]
    tm = min(FFN_TILE, t)
    assert t % tm == 0
    tok = lambda cols: pl.BlockSpec((tm, cols), lambda i: (i, 0))
    in_specs = [tok(D_MODEL), _layer_spec(layer, 1, D_MODEL), _layer_spec(layer, D_MODEL, D_FF),
                _layer_spec(layer, D_MODEL, D_FF), _layer_spec(layer, D_FF, D_MODEL)]
    args = [h, norm, wg, wu, wd]
    ple = ple_args is not None
    if ple:
        p, pnorm, pwg, pwp = ple_args
        in_specs += [pl.BlockSpec((None, tm, PLE_DIM), lambda i: (layer, i, 0)),
                     _layer_spec(layer, 1, D_MODEL), _layer_spec(layer, D_MODEL, D_MODEL),
                     _layer_spec(layer, PLE_DIM, D_MODEL),
                     pl.BlockSpec((1, D_MODEL), lambda i: (0, 0))]
        args += [p, pnorm, pwg, pwp, final_norm]
    return pl.pallas_call(
        functools.partial(_ffn_kernel, ple=ple, final=ple and layer == norm.shape[0] - 1),
        grid=(t // tm,),
        in_specs=in_specs,
        out_specs=tok(D_MODEL),
        out_shape=jax.ShapeDtypeStruct((t, D_MODEL), f32),
        compiler_params=pltpu.CompilerParams(dimension_semantics=("arbitrary",),
                                             vmem_limit_bytes=VMEM_LIMIT),
        name=f"ffn_ple{int(ple)}",
    )(*args)


TILES_PER_CHUNK = SSM_CHUNK // SUBLANES
HALO_TILES = SSM_CONV - 1
HALO_ROWS = HALO_TILES * SUBLANES
CHUNK_PITCH = HALO_ROWS + SSM_CHUNK
N_SC_CHUNKS = SC_WIDTH // COL_CHUNK
N_M_CHUNKS = SSM_CONV_DIM // COL_CHUNK


def _time_of_row(q):
    in_chunk = q & (SSM_CHUNK - 1)
    return (q - in_chunk) + (in_chunk & (SUBLANES - 1)) * TILES_PER_CHUNK + lax.shift_right_logical(
        in_chunk, SUBLANES.bit_length() - 1)


def _time_permutation(ts, inverse=False):
    rows = lax.broadcasted_iota(jnp.int32, (ts, ts), 0)
    cols = lax.broadcasted_iota(jnp.int32, (ts, ts), 1)
    hit = (rows == _time_of_row(cols)) if inverse else (_time_of_row(rows) == cols)
    return hit.astype(bf16)


def _chunk_data(c):
    return HALO_ROWS + c * CHUNK_PITCH + HALO_ROWS


def _stage(buf_ref, x):
    n_chunks = x.shape[0] // SSM_CHUNK
    tail = SSM_CHUNK - HALO_ROWS
    first_sublane = lax.broadcasted_iota(jnp.int32, (SUBLANES, x.shape[1]), 0) == 0
    for c in range(n_chunks):
        buf_ref[_chunk_data(c):_chunk_data(c) + SSM_CHUNK, :] = x[c * SSM_CHUNK:(c + 1) * SSM_CHUNK, :]
    for c in range(n_chunks):
        prev0 = 0 if c == 0 else _chunk_data(c - 1) + tail
        cur0 = _chunk_data(c) + tail
        halo0 = _chunk_data(c) - HALO_ROWS
        for m in range(HALO_TILES):
            tile = slice(m * SUBLANES, (m + 1) * SUBLANES)
            cur = buf_ref[cur0 + tile.start:cur0 + tile.stop, :]
            prev = buf_ref[prev0 + tile.start:prev0 + tile.stop, :]
            buf_ref[halo0 + tile.start:halo0 + tile.stop, :] = jnp.where(
                first_sublane, pltpu.roll(prev, 1, axis=0), pltpu.roll(cur, 1, axis=0))


def _carry(buf_ref, ts):
    last = _chunk_data(ts // SSM_CHUNK - 1) + SSM_CHUNK - HALO_ROWS
    buf_ref[0:HALO_ROWS, :] = buf_ref[last:last + HALO_ROWS, :]


def _conv_block(buf_ref, w_ref, w0, taps, r, wcols):
    base = _chunk_data(r // SSM_CHUNK) + r % SSM_CHUNK
    out = None
    for k in range(taps):
        off = base - (taps - 1 - k) * SUBLANES
        term = buf_ref[off:off + ROW_BLOCK, :] * w_ref[w0 + k * ROW_BLOCK:w0 + (k + 1) * ROW_BLOCK, wcols]
        out = term if out is None else out + term
    return out


def _mix_in(h_ref, pm_ref, w_ref, mid_ref, dt_ref, perm_ref, xnat_ref, xn_ref, bufs):
    scbufs, mbufs = bufs[:N_SC_CHUNKS], bufs[N_SC_CHUNKS:]
    ts = h_ref.shape[0]
    row_blocks = range(0, ts, ROW_BLOCK)

    gain = pm_ref[PM_NORM:PM_NORM + 1, 0:D_MODEL]
    for r in row_blocks:
        rows = slice(r, r + ROW_BLOCK)
        xnat_ref[rows, :] = _rms(h_ref[rows, :], gain).astype(bf16)
    xn_ref[...] = _dot(perm_ref[...], xnat_ref[...]).astype(bf16)

    def proj(c0, width=COL_CHUNK):
        return _dot(xn_ref[...], w_ref[:, c0:c0 + width])

    for i in range(N_SC_CHUNKS):
        prods = []
        for c0 in range(2 * i * COL_CHUNK, 2 * (i + 1) * COL_CHUNK, COL_CHUNK):
            cx = proj(COL_CX + c0)
            prods.append(cx[:, :CX_TILE] * cx[:, CX_TILE:])
        _stage(scbufs[i], jnp.concatenate(prods, axis=1))

    def sc_job(i):
        cols = slice(i * COL_CHUNK, (i + 1) * COL_CHUNK)
        scb = proj(COL_SCB + i * COL_CHUNK)
        for r in row_blocks:
            conv = _conv_block(scbufs[i], pm_ref, PM_SCW, SC_KERNEL, r, cols)
            mid_ref[r:r + ROW_BLOCK, MID_SCP + cols.start:MID_SCP + cols.stop] = (
                scb[r:r + ROW_BLOCK, :] * conv).astype(bf16)
        _carry(scbufs[i], ts)

    def m_start(i):
        _stage(mbufs[i], proj(COL_XBC + i * COL_CHUNK))

    def m_finish(i):
        cols = slice(i * COL_CHUNK, (i + 1) * COL_CHUNK)
        for r in row_blocks:
            conv = (_conv_block(mbufs[i], pm_ref, PM_MCW, SSM_CONV, r, cols)
                    + pm_ref[PM_MCB:PM_MCB + ROW_BLOCK, cols])
            mid_ref[r:r + ROW_BLOCK, MID_XS + cols.start:MID_XS + cols.stop] = _silu(conv).astype(bf16)
        _carry(mbufs[i], ts)

    def z_job(i):
        c0 = i * COL_CHUNK
        mid_ref[:, MID_Z + c0:MID_Z + c0 + COL_CHUNK] = proj(COL_Z + c0).astype(bf16)

    def gate_job(i):
        c0 = i * COL_CHUNK
        mid_ref[:, MID_GT + c0:MID_GT + c0 + COL_CHUNK] = _sigmoid(proj(COL_GATE + c0)).astype(bf16)

    light = ([(z_job, i) for i in range(SSM_INNER // COL_CHUNK)]
             + [(gate_job, i) for i in range(2 * D_MODEL // COL_CHUNK)])
    for i in range(N_SC_CHUNKS):
        sc_job(i)
        job, j = light.pop(0)
        job(j)
    m_start(0)
    for i in range(N_M_CHUNKS):
        if i + 1 < N_M_CHUNKS:
            m_start(i + 1)
        job, j = light.pop(0)
        job(j)
        m_finish(i)
    assert not light
    dt_ref[...] = jax.nn.softplus(proj(COL_DT, LANES) + pm_ref[PM_DTB:PM_DTB + 1, 0:LANES])


def _chunk_time(q):
    return (q & (SUBLANES - 1)) * TILES_PER_CHUNK + lax.shift_right_logical(q, SUBLANES.bit_length() - 1)


def _ssd_chunk(rows, mid_ref, dt_ref, a_row, d_row, expand_ref, st_ref, y_ref):
    L = SSM_CHUNK
    nbc = SSM_GROUPS * SSM_STATE
    dt = dt_ref[rows, :]
    a = dt * a_row
    causal = (_chunk_time(lax.broadcasted_iota(jnp.int32, (L, L), 0))
              >= _chunk_time(lax.broadcasted_iota(jnp.int32, (L, L), 1)))
    tril = causal.astype(bf16)
    acum = _dot(jnp.concatenate([tril] * 3, axis=1), jnp.concatenate(_split3(a), axis=0))
    acum2 = acum * LOG2E
    acum2_t = acum2.T
    dt_t = dt.T
    a_last = acum[L - 1:L, :]
    w = jnp.exp(a_last - acum) * dt
    stacked = jnp.concatenate([w, jnp.exp(acum)], axis=0)
    expd = _dot(jnp.concatenate(_split2(stacked), axis=1), expand_ref[...])
    lane = lax.broadcasted_iota(jnp.int32, (L, LANES), 1)
    low_half = lane < SSM_HEADDIM
    for g in range(SSM_GROUPS):
        cs = slice(g * GROUP_WIDTH, (g + 1) * GROUP_WIDTH)
        bg = mid_ref[rows, MID_BC + g * SSM_STATE:MID_BC + (g + 1) * SSM_STATE]
        cg = mid_ref[rows, MID_BC + nbc + g * SSM_STATE:MID_BC + nbc + (g + 1) * SSM_STATE]
        st = st_ref[:, cs]
        y_off = _dot(cg, st.astype(bf16)) * expd[L:, cs]
        xw = (mid_ref[rows, MID_XS + cs.start:MID_XS + cs.stop].astype(f32) * expd[:L, cs]).astype(bf16)
        new = lax.dot_general(bg, xw, (((0,), (0,)), ((), ())), preferred_element_type=f32)
        st_ref[:, cs] = st * expd[2 * L - 1:2 * L, cs] + new
        cb = lax.dot_general(cg, bg, (((1,), (1,)), ((), ())), preferred_element_type=f32)
        for pr in range(HEADS_PER_GROUP // 2):
            ms = []
            for k in range(2):
                hd = g * HEADS_PER_GROUP + 2 * pr + k
                diff = acum2[:, hd:hd + 1] - acum2_t[hd:hd + 1, :]
                decay = jnp.exp2(jnp.where(causal, diff, -jnp.inf))
                ms.append((cb * decay * dt_t[hd:hd + 1, :]).astype(bf16))
            pc = slice(g * GROUP_WIDTH + pr * LANES, g * GROUP_WIDTH + (pr + 1) * LANES)
            xp = mid_ref[rows, MID_XS + pc.start:MID_XS + pc.stop]
            zero = jnp.zeros_like(xp)
            rhs = jnp.concatenate([jnp.where(low_half, xp, zero), jnp.where(low_half, zero, xp)], axis=0)
            y_ref[rows, pc] = (_dot(jnp.concatenate(ms, axis=1), rhs)
                               + y_off[:, pr * LANES:(pr + 1) * LANES] + d_row[:, pc] * xp.astype(f32))


def _mix_out(h_ref, mid_ref, dt_ref, sp_ref, w_ref, o_ref,
             expand_ref, unperm_ref, st_ref, y_ref, yn_ref, mg_ref, mgn_ref):
    ts = h_ref.shape[0]
    a_row = -jnp.exp(sp_ref[SP_ALOG:SP_ALOG + 1, 0:LANES])
    d_row = sp_ref[SP_D:SP_D + 1, :]
    for c in range(ts // SSM_CHUNK):
        _ssd_chunk(slice(c * SSM_CHUNK, (c + 1) * SSM_CHUNK), mid_ref, dt_ref, a_row, d_row,
                   expand_ref, st_ref, y_ref)

    for g in range(SSM_GROUPS):
        cs = slice(g * GROUP_WIDTH, (g + 1) * GROUP_WIDTH)
        for r in range(0, ts, ROW_BLOCK):
            rows = slice(r, r + ROW_BLOCK)
            yz = y_ref[rows, cs] * _silu(mid_ref[rows, MID_Z + cs.start:MID_Z + cs.stop].astype(f32))
            yn_ref[rows, cs] = _rms(yz, sp_ref[SP_NORM:SP_NORM + 1, cs]).astype(bf16)
    for c0 in range(0, D_MODEL, COL_CHUNK):
        cols = slice(c0, c0 + COL_CHUNK)
        y_a = _dot(mid_ref[:, MID_SCP:MID_SCP + SC_WIDTH], w_ref[WO_SC:WO_SC + SC_WIDTH, cols])
        y_m = _dot(yn_ref[...], w_ref[WO_M:WO_M + SSM_INNER, cols])
        gate_a = mid_ref[:, MID_GT + c0:MID_GT + c0 + COL_CHUNK].astype(f32)
        gate_m = mid_ref[:, MID_GT + D_MODEL + c0:MID_GT + D_MODEL + c0 + COL_CHUNK].astype(f32)
        mg_ref[:, cols] = (gate_a * y_a + gate_m * y_m).astype(bf16)
    mgn_ref[...] = _dot(unperm_ref[...], mg_ref[...]).astype(bf16)
    for c0 in range(0, D_MODEL, COL_CHUNK):
        cols = slice(c0, c0 + COL_CHUNK)
        o_ref[:, cols] = h_ref[:, cols] + _dot(mgn_ref[...], w_ref[WO_O:WO_O + D_MODEL, cols])


def _mixer_kernel(h_ref, pm_ref, w_ref, sp_ref, wo_ref, o_ref,
                  perm_ref, unperm_ref, expand_ref, xnat_ref, xn_ref, mid_ref, dt_ref,
                  st_ref, y_ref, yn_ref, mg_ref, mgn_ref, *bufs):
    ts = h_ref.shape[0]

    @pl.when((pl.program_id(0) == 0) & (pl.program_id(1) == 0))
    def _():
        perm_ref[...] = _time_permutation(ts)
        unperm_ref[...] = _time_permutation(ts, inverse=True)
        head_of_row = lax.broadcasted_iota(jnp.int32, expand_ref.shape, 0) & (LANES - 1)
        head_of_channel = lax.shift_right_logical(lax.broadcasted_iota(jnp.int32, expand_ref.shape, 1),
                                                  SSM_HEADDIM.bit_length() - 1)
        expand_ref[...] = (head_of_row == head_of_channel).astype(bf16)

    @pl.when(pl.program_id(1) == 0)
    def _():
        for buf in bufs:
            buf[0:HALO_ROWS, :] = jnp.zeros((HALO_ROWS, COL_CHUNK), f32)
        st_ref[...] = jnp.zeros(st_ref.shape, f32)

    _mix_in(h_ref, pm_ref, w_ref, mid_ref, dt_ref, perm_ref, xnat_ref, xn_ref, bufs)
    _mix_out(h_ref, mid_ref, dt_ref, sp_ref, wo_ref, o_ref,
             expand_ref, unperm_ref, st_ref, y_ref, yn_ref, mg_ref, mgn_ref)


def _mixer(h, layer, in_params, w_all, out_params, w_out):
    b, s, _ = h.shape
    ts = min(MIX_TILE, s)
    assert s % ts == 0 and ts % SSM_CHUNK == 0
    tok = pl.BlockSpec((None, ts, D_MODEL), lambda bi, j: (bi, j, 0))
    return pl.pallas_call(
        _mixer_kernel,
        grid=(b, s // ts),
        in_specs=[tok, _layer_spec(layer, PM_ROWS, SSM_CONV_DIM), _layer_spec(layer, D_MODEL, PROJ_COLS),
                  _layer_spec(layer, SP_ROWS, SSM_INNER), _layer_spec(layer, WO_ROWS, D_MODEL)],
        out_specs=tok,
        out_shape=jax.ShapeDtypeStruct((b, s, D_MODEL), f32),
        scratch_shapes=[pltpu.VMEM((ts, ts), bf16), pltpu.VMEM((ts, ts), bf16),
                        pltpu.VMEM((2 * LANES, SSM_INNER), bf16),
                        pltpu.VMEM((ts, D_MODEL), bf16), pltpu.VMEM((ts, D_MODEL), bf16),
                        pltpu.VMEM((ts, MID_COLS), bf16), pltpu.VMEM((ts, LANES), f32),
                        pltpu.VMEM((SSM_STATE, SSM_INNER), f32), pltpu.VMEM((ts, SSM_INNER), f32),
                        pltpu.VMEM((ts, SSM_INNER), bf16), pltpu.VMEM((ts, D_MODEL), bf16),
                        pltpu.VMEM((ts, D_MODEL), bf16)]
        + [pltpu.VMEM((HALO_ROWS + ts // SSM_CHUNK * CHUNK_PITCH, COL_CHUNK), f32)] * (N_SC_CHUNKS + N_M_CHUNKS),
        compiler_params=pltpu.CompilerParams(dimension_semantics=("arbitrary", "arbitrary"),
                                             vmem_limit_bytes=VMEM_LIMIT),
        name="mixer",
    )(h, in_params, w_all, out_params, w_out)


def _row(x):
    return x.reshape(x.shape[0], 1, x.shape[1])


def _rows(x, n_rows, width):
    return jnp.pad(x, ((0, 0), (0, n_rows - x.shape[1]), (0, width - x.shape[2])))


def _arrange_w_in(w_in):
    w = w_in.astype(bf16)
    cx_tiles = []
    for c0 in range(0, SC_WIDTH, CX_TILE):
        cx_tiles += [w[..., SC_WIDTH + c0:SC_WIDTH + c0 + CX_TILE],
                     w[..., 2 * SC_WIDTH + c0:2 * SC_WIDTH + c0 + CX_TILE]]
    return jnp.concatenate(
        [w[..., :SC_WIDTH], *cx_tiles, w[..., 3 * SC_WIDTH:DT_SRC], w[..., DT_SRC + SSM_HEADS:],
         w[..., DT_SRC:DT_SRC + SSM_HEADS], jnp.zeros(w.shape[:2] + (LANES - SSM_HEADS,), bf16)], axis=-1)


def kernel(x, p, ffn1_norm, ffn1_wg, ffn1_wu, ffn1_wd, mix_norm, w_in, sc_conv_w, sc_w_out, m_conv_w, m_conv_b, m_dt_bias, m_A_log, m_D, m_norm, m_w_out, w_o, ffn2_norm, ffn2_wg, ffn2_wu, ffn2_wd, ple_norm, ple_w_gate, ple_w_proj, final_norm):
    depth = w_in.shape[0]
    b, s, d = x.shape
    t = b * s
    cast = lambda w: w.astype(bf16)
    f1 = (_row(ffn1_norm), cast(ffn1_wg), cast(ffn1_wu), cast(ffn1_wd))
    f2 = (_row(ffn2_norm), cast(ffn2_wg), cast(ffn2_wu), cast(ffn2_wd))
    ple_w = (_row(ple_norm), cast(ple_w_gate), cast(ple_w_proj))
    rep = lambda w: jnp.repeat(w, ROW_BLOCK, axis=1)
    wide = lambda w, n_rows: _rows(w, n_rows, SSM_CONV_DIM)
    mix_in_p = jnp.concatenate(
        [rep(_row(m_conv_b)), rep(m_conv_w), wide(rep(sc_conv_w), SC_KERNEL * ROW_BLOCK),
         wide(_row(mix_norm), SUBLANES), wide(_row(m_dt_bias), SUBLANES)], axis=1)
    mix_out_p = jnp.concatenate(
        [_row(jnp.repeat(m_D, SSM_HEADDIM, axis=1)), _row(m_norm),
         _rows(_row(m_A_log), SP_ROWS - 2, SSM_INNER)], axis=1)
    w_all = _arrange_w_in(w_in)
    w_out = jnp.concatenate([cast(m_w_out), cast(sc_w_out), cast(w_o)], axis=1)
    p2 = p.reshape(depth, t, PLE_DIM)
    fnorm = final_norm.reshape(1, d)

    h = x.reshape(t, d)
    for i in range(depth):
        h = _ffn(h, i, *f1)
        h = _mixer(h.reshape(b, s, d), i, mix_in_p, w_all, mix_out_p, w_out).reshape(t, d)
        h = _ffn(h, i, *f2, ple_args=(p2,) + ple_w, final_norm=fnorm)
    return h.reshape(b, s, d)
```

```python
import functools

import jax
import jax.numpy as jnp
from jax import lax
from jax.experimental import pallas as pl
from jax.experimental.pallas import tpu as pltpu

D_MODEL = 1024
D_FF = 2816
PLE_DIM = 256
SC_WIDTH = 1024
SC_KERNEL = 3
SSM_INNER = 2048
SSM_HEADDIM = 64
SSM_HEADS = 32
SSM_GROUPS = 4
SSM_STATE = 128
SSM_CONV = 4
SSM_CHUNK = 128
SSM_CONV_DIM = SSM_INNER + 2 * SSM_GROUPS * SSM_STATE
GROUP_WIDTH = SSM_INNER // SSM_GROUPS
HEADS_PER_GROUP = SSM_HEADS // SSM_GROUPS
EPS = 1e-6
LOG2E = 1.4426950408889634

LANES = 128
SUBLANES = 8
ROW_BLOCK = 16
COL_CHUNK = 512

CX_TILE = COL_CHUNK // 2
COL_SCB = 0
COL_CX = SC_WIDTH
COL_Z = 3 * SC_WIDTH
COL_XBC = COL_Z + SSM_INNER
COL_GATE = COL_XBC + SSM_CONV_DIM
COL_DT = COL_GATE + 2 * D_MODEL
PROJ_COLS = COL_DT + LANES
DT_SRC = COL_GATE

MID_SCP = 0
MID_Z = MID_SCP + SC_WIDTH
MID_XS = MID_Z + SSM_INNER
MID_BC = MID_XS + SSM_INNER
MID_GT = MID_BC + 2 * SSM_GROUPS * SSM_STATE
MID_COLS = MID_GT + 2 * D_MODEL

PM_MCB = 0
PM_MCW = PM_MCB + ROW_BLOCK
PM_SCW = PM_MCW + SSM_CONV * ROW_BLOCK
PM_NORM = PM_SCW + SC_KERNEL * ROW_BLOCK
PM_DTB = PM_NORM + SUBLANES
PM_ROWS = PM_DTB + SUBLANES

SP_D, SP_NORM, SP_ALOG, SP_ROWS = 0, 1, 2, SUBLANES
WO_M, WO_SC, WO_O = 0, SSM_INNER, SSM_INNER + SC_WIDTH
WO_ROWS = WO_O + D_MODEL

FF_SPLITS = tuple((c, min(c + 512, D_FF)) for c in range(0, D_FF, 512))

FFN_TILE = 1024
MIX_TILE = 256
VMEM_LIMIT = 56 * 1024 * 1024

bf16 = jnp.bfloat16
f32 = jnp.float32


def _rms(x, g):
    return x * lax.rsqrt(jnp.mean(x * x, axis=-1, keepdims=True) + EPS) * g


def _sigmoid(x):
    return 0.5 * jnp.tanh(0.5 * x) + 0.5


def _silu(x):
    return x * _sigmoid(x)


def _dot(a, b):
    return jnp.dot(a, b, preferred_element_type=f32)


def _split2(x):
    hi = x.astype(bf16)
    lo = (x - hi.astype(f32)).astype(bf16)
    return hi, lo


def _split3(x):
    hi = x.astype(bf16)
    r = x - hi.astype(f32)
    mid = r.astype(bf16)
    lo = (r - mid.astype(f32)).astype(bf16)
    return hi, mid, lo


def _ffn_kernel(*refs, ple, final):
    if ple:
        (h_ref, g_ref, wg_ref, wu_ref, wd_ref, p_ref, pg_ref, pwg_ref, pwp_ref, fg_ref, o_ref) = refs
    else:
        (h_ref, g_ref, wg_ref, wu_ref, wd_ref, o_ref) = refs
    h = h_ref[...]
    xn = _rms(h, g_ref[...]).astype(bf16)
    acc = None
    for lo, hi in FF_SPLITS:
        gate = _dot(xn, wg_ref[:, lo:hi])
        up = _dot(xn, wu_ref[:, lo:hi])
        act = (_silu(gate) * up).astype(bf16)
        part = _dot(act, wd_ref[lo:hi, :])
        acc = part if acc is None else acc + part
    h = h + 0.5 * acc
    if ple:
        gate = _sigmoid(_dot(_rms(h, pg_ref[...]).astype(bf16), pwg_ref[...]))
        h = h + gate * _dot(p_ref[...].astype(bf16), pwp_ref[...])
    if final:
        h = _rms(h, fg_ref[...])
    o_ref[...] = h


def _layer_spec(layer, rows, cols):
    return pl.BlockSpec((None, rows, cols), lambda *_: (layer, 0, 0), pipeline_mode=pl.Buffered(1))


def _ffn(h, layer, norm, wg, wu, wd, ple_args=None, final_norm=None):
    t = h.shape[0]
    tm = min(FFN_TILE, t)
    assert t % tm == 0
    tok = lambda cols: pl.BlockSpec((tm, cols), lambda i: (i, 0))
    in_specs = [tok(D_MODEL), _layer_spec(layer, 1, D_MODEL), _layer_spec(layer, D_MODEL, D_FF),
                _layer_spec(layer, D_MODEL, D_FF), _layer_spec(layer, D_FF, D_MODEL)]
    args = [h, norm, wg, wu, wd]
    ple = ple_args is not None
    if ple:
        p, pnorm, pwg, pwp = ple_args
        in_specs += [pl.BlockSpec((None, tm, PLE_DIM), lambda i: (layer, i, 0)),
                     _layer_spec(layer, 1, D_MODEL), _layer_spec(layer, D_MODEL, D_MODEL),
                     _layer_spec(layer, PLE_DIM, D_MODEL),
                     pl.BlockSpec((1, D_MODEL), lambda i: (0, 0))]
        args += [p, pnorm, pwg, pwp, final_norm]
    return pl.pallas_call(
        functools.partial(_ffn_kernel, ple=ple, final=ple and layer == norm.shape[0] - 1),
        grid=(t // tm,),
        in_specs=in_specs,
        out_specs=tok(D_MODEL),
        out_shape=jax.ShapeDtypeStruct((t, D_MODEL), f32),
        compiler_params=pltpu.CompilerParams(dimension_semantics=("arbitrary",),
                                             vmem_limit_bytes=VMEM_LIMIT),
        name=f"ffn_ple{int(ple)}",
    )(*args)


TILES_PER_CHUNK = SSM_CHUNK // SUBLANES
HALO_TILES = SSM_CONV - 1
HALO_ROWS = HALO_TILES * SUBLANES
CHUNK_PITCH = HALO_ROWS + SSM_CHUNK
N_SC_CHUNKS = SC_WIDTH // COL_CHUNK
N_M_CHUNKS = SSM_CONV_DIM // COL_CHUNK


def _time_of_row(q):
    in_chunk = q & (SSM_CHUNK - 1)
    return (q - in_chunk) + (in_chunk & (SUBLANES - 1)) * TILES_PER_CHUNK + lax.shift_right_logical(
        in_chunk, SUBLANES.bit_length() - 1)


def _time_permutation(ts, inverse=False):
    rows = lax.broadcasted_iota(jnp.int32, (ts, ts), 0)
    cols = lax.broadcasted_iota(jnp.int32, (ts, ts), 1)
    hit = (rows == _time_of_row(cols)) if inverse else (_time_of_row(rows) == cols)
    return hit.astype(bf16)


def _chunk_data(c):
    return HALO_ROWS + c * CHUNK_PITCH + HALO_ROWS


def _stage(buf_ref, x):
    n_chunks = x.shape[0] // SSM_CHUNK
    tail = SSM_CHUNK - HALO_ROWS
    first_sublane = lax.broadcasted_iota(jnp.int32, (SUBLANES, x.shape[1]), 0) == 0
    for c in range(n_chunks):
        buf_ref[_chunk_data(c):_chunk_data(c) + SSM_CHUNK, :] = x[c * SSM_CHUNK:(c + 1) * SSM_CHUNK, :]
    for c in range(n_chunks):
        prev0 = 0 if c == 0 else _chunk_data(c - 1) + tail
        cur0 = _chunk_data(c) + tail
        halo0 = _chunk_data(c) - HALO_ROWS
        for m in range(HALO_TILES):
            tile = slice(m * SUBLANES, (m + 1) * SUBLANES)
            cur = buf_ref[cur0 + tile.start:cur0 + tile.stop, :]
            prev = buf_ref[prev0 + tile.start:prev0 + tile.stop, :]
            buf_ref[halo0 + tile.start:halo0 + tile.stop, :] = jnp.where(
                first_sublane, pltpu.roll(prev, 1, axis=0), pltpu.roll(cur, 1, axis=0))


def _carry(buf_ref, ts):
    last = _chunk_data(ts // SSM_CHUNK - 1) + SSM_CHUNK - HALO_ROWS
    buf_ref[0:HALO_ROWS, :] = buf_ref[last:last + HALO_ROWS, :]


def _conv_block(buf_ref, w_ref, w0, taps, r, wcols):
    base = _chunk_data(r // SSM_CHUNK) + r % SSM_CHUNK
    out = None
    for k in range(taps):
        off = base - (taps - 1 - k) * SUBLANES
        term = buf_ref[off:off + ROW_BLOCK, :] * w_ref[w0 + k * ROW_BLOCK:w0 + (k + 1) * ROW_BLOCK, wcols]
        out = term if out is None else out + term
    return out


def _mix_in(h_ref, pm_ref, w_ref, mid_ref, dt_ref, perm_ref, xnat_ref, xn_ref, bufs):
    scbufs, mbufs = bufs[:N_SC_CHUNKS], bufs[N_SC_CHUNKS:]
    ts = h_ref.shape[0]
    row_blocks = range(0, ts, ROW_BLOCK)

    gain = pm_ref[PM_NORM:PM_NORM + 1, 0:D_MODEL]
    for r in row_blocks:
        rows = slice(r, r + ROW_BLOCK)
        xnat_ref[rows, :] = _rms(h_ref[rows, :], gain).astype(bf16)
    xn_ref[...] = _dot(perm_ref[...], xnat_ref[...]).astype(bf16)

    def proj(c0, width=COL_CHUNK):
        return _dot(xn_ref[...], w_ref[:, c0:c0 + width])

    for i in range(N_SC_CHUNKS):
        prods = []
        for c0 in range(2 * i * COL_CHUNK, 2 * (i + 1) * COL_CHUNK, COL_CHUNK):
            cx = proj(COL_CX + c0)
            prods.append(cx[:, :CX_TILE] * cx[:, CX_TILE:])
        _stage(scbufs[i], jnp.concatenate(prods, axis=1))

    def sc_job(i):
        cols = slice(i * COL_CHUNK, (i + 1) * COL_CHUNK)
        scb = proj(COL_SCB + i * COL_CHUNK)
        for r in row_blocks:
            conv = _conv_block(scbufs[i], pm_ref, PM_SCW, SC_KERNEL, r, cols)
            mid_ref[r:r + ROW_BLOCK, MID_SCP + cols.start:MID_SCP + cols.stop] = (
                scb[r:r + ROW_BLOCK, :] * conv).astype(bf16)
        _carry(scbufs[i], ts)

    def m_start(i):
        _stage(mbufs[i], proj(COL_XBC + i * COL_CHUNK))

    def m_finish(i):
        cols = slice(i * COL_CHUNK, (i + 1) * COL_CHUNK)
        for r in row_blocks:
            conv = (_conv_block(mbufs[i], pm_ref, PM_MCW, SSM_CONV, r, cols)
                    + pm_ref[PM_MCB:PM_MCB + ROW_BLOCK, cols])
            mid_ref[r:r + ROW_BLOCK, MID_XS + cols.start:MID_XS + cols.stop] = _silu(conv).astype(bf16)
        _carry(mbufs[i], ts)

    def z_job(i):
        c0 = i * COL_CHUNK
        mid_ref[:, MID_Z + c0:MID_Z + c0 + COL_CHUNK] = proj(COL_Z + c0).astype(bf16)

    def gate_job(i):
        c0 = i * COL_CHUNK
        mid_ref[:, MID_GT + c0:MID_GT + c0 + COL_CHUNK] = _sigmoid(proj(COL_GATE + c0)).astype(bf16)

    light = ([(z_job, i) for i in range(SSM_INNER // COL_CHUNK)]
             + [(gate_job, i) for i in range(2 * D_MODEL // COL_CHUNK)])
    for i in range(N_SC_CHUNKS):
        sc_job(i)
        job, j = light.pop(0)
        job(j)
    m_start(0)
    for i in range(N_M_CHUNKS):
        if i + 1 < N_M_CHUNKS:
            m_start(i + 1)
        job, j = light.pop(0)
        job(j)
        m_finish(i)
    assert not light
    dt_ref[...] = jax.nn.softplus(proj(COL_DT, LANES) + pm_ref[PM_DTB:PM_DTB + 1, 0:LANES])


def _chunk_time(q):
    return (q & (SUBLANES - 1)) * TILES_PER_CHUNK + lax.shift_right_logical(q, SUBLANES.bit_length() - 1)


def _ssd_chunk(rows, mid_ref, dt_ref, a_row, d_row, expand_ref, st_ref, y_ref):
    L = SSM_CHUNK
    nbc = SSM_GROUPS * SSM_STATE
    dt = dt_ref[rows, :]
    a = dt * a_row
    causal = (_chunk_time(lax.broadcasted_iota(jnp.int32, (L, L), 0))
              >= _chunk_time(lax.broadcasted_iota(jnp.int32, (L, L), 1)))
    tril = causal.astype(bf16)
    acum = _dot(jnp.concatenate([tril] * 3, axis=1), jnp.concatenate(_split3(a), axis=0))
    acum2 = acum * LOG2E
    acum2_t = acum2.T
    dt_t = dt.T
    a_last = acum[L - 1:L, :]
    w = jnp.exp(a_last - acum) * dt
    stacked = jnp.concatenate([w, jnp.exp(acum)], axis=0)
    expd = _dot(jnp.concatenate(_split2(stacked), axis=1), expand_ref[...])
    lane = lax.broadcasted_iota(jnp.int32, (L, LANES), 1)
    low_half = lane < SSM_HEADDIM
    for g in range(SSM_GROUPS):
        cs = slice(g * GROUP_WIDTH, (g + 1) * GROUP_WIDTH)
        bg = mid_ref[rows, MID_BC + g * SSM_STATE:MID_BC + (g + 1) * SSM_STATE]
        cg = mid_ref[rows, MID_BC + nbc + g * SSM_STATE:MID_BC + nbc + (g + 1) * SSM_STATE]
        st = st_ref[:, cs]
        y_off = _dot(cg, st.astype(bf16)) * expd[L:, cs]
        xw = (mid_ref[rows, MID_XS + cs.start:MID_XS + cs.stop].astype(f32) * expd[:L, cs]).astype(bf16)
        new = lax.dot_general(bg, xw, (((0,), (0,)), ((), ())), preferred_element_type=f32)
        st_ref[:, cs] = st * expd[2 * L - 1:2 * L, cs] + new
        cb = lax.dot_general(cg, bg, (((1,), (1,)), ((), ())), preferred_element_type=f32)
        for pr in range(HEADS_PER_GROUP // 2):
            ms = []
            for k in range(2):
                hd = g * HEADS_PER_GROUP + 2 * pr + k
                diff = acum2[:, hd:hd + 1] - acum2_t[hd:hd + 1, :]
                decay = jnp.exp2(jnp.where(causal, diff, -jnp.inf))
                ms.append((cb * decay * dt_t[hd:hd + 1, :]).astype(bf16))
            pc = slice(g * GROUP_WIDTH + pr * LANES, g * GROUP_WIDTH + (pr + 1) * LANES)
            xp = mid_ref[rows, MID_XS + pc.start:MID_XS + pc.stop]
            zero = jnp.zeros_like(xp)
            rhs = jnp.concatenate([jnp.where(low_half, xp, zero), jnp.where(low_half, zero, xp)], axis=0)
            y_ref[rows, pc] = (_dot(jnp.concatenate(ms, axis=1), rhs)
                               + y_off[:, pr * LANES:(pr + 1) * LANES] + d_row[:, pc] * xp.astype(f32))


def _mix_out(h_ref, mid_ref, dt_ref, sp_ref, w_ref, o_ref,
             expand_ref, unperm_ref, st_ref, y_ref, yn_ref, mg_ref, mgn_ref):
    ts = h_ref.shape[0]
    a_row = -jnp.exp(sp_ref[SP_ALOG:SP_ALOG + 1, 0:LANES])
    d_row = sp_ref[SP_D:SP_D + 1, :]
    for c in range(ts // SSM_CHUNK):
        _ssd_chunk(slice(c * SSM_CHUNK, (c + 1) * SSM_CHUNK), mid_ref, dt_ref, a_row, d_row,
                   expand_ref, st_ref, y_ref)

    for g in range(SSM_GROUPS):
        cs = slice(g * GROUP_WIDTH, (g + 1) * GROUP_WIDTH)
        for r in range(0, ts, ROW_BLOCK):
            rows = slice(r, r + ROW_BLOCK)
            yz = y_ref[rows, cs] * _silu(mid_ref[rows, MID_Z + cs.start:MID_Z + cs.stop].astype(f32))
            yn_ref[rows, cs] = _rms(yz, sp_ref[SP_NORM:SP_NORM + 1, cs]).astype(bf16)
    for c0 in range(0, D_MODEL, COL_CHUNK):
        cols = slice(c0, c0 + COL_CHUNK)
        y_a = _dot(mid_ref[:, MID_SCP:MID_SCP + SC_WIDTH], w_ref[WO_SC:WO_SC + SC_WIDTH, cols])
        y_m = _dot(yn_ref[...], w_ref[WO_M:WO_M + SSM_INNER, cols])
        gate_a = mid_ref[:, MID_GT + c0:MID_GT + c0 + COL_CHUNK].astype(f32)
        gate_m = mid_ref[:, MID_GT + D_MODEL + c0:MID_GT + D_MODEL + c0 + COL_CHUNK].astype(f32)
        mg_ref[:, cols] = (gate_a * y_a + gate_m * y_m).astype(bf16)
    mgn_ref[...] = _dot(unperm_ref[...], mg_ref[...]).astype(bf16)
    for c0 in range(0, D_MODEL, COL_CHUNK):
        cols = slice(c0, c0 + COL_CHUNK)
        o_ref[:, cols] = h_ref[:, cols] + _dot(mgn_ref[...], w_ref[WO_O:WO_O + D_MODEL, cols])


def _mixer_kernel(h_ref, pm_ref, w_ref, sp_ref, wo_ref, o_ref,
                  perm_ref, unperm_ref, expand_ref, xnat_ref, xn_ref, mid_ref, dt_ref,
                  st_ref, y_ref, yn_ref, mg_ref, mgn_ref, *bufs):
    ts = h_ref.shape[0]

    @pl.when((pl.program_id(0) == 0) & (pl.program_id(1) == 0))
    def _():
        perm_ref[...] = _time_permutation(ts)
        unperm_ref[...] = _time_permutation(ts, inverse=True)
        head_of_row = lax.broadcasted_iota(jnp.int32, expand_ref.shape, 0) & (LANES - 1)
        head_of_channel = lax.shift_right_logical(lax.broadcasted_iota(jnp.int32, expand_ref.shape, 1),
                                                  SSM_HEADDIM.bit_length() - 1)
        expand_ref[...] = (head_of_row == head_of_channel).astype(bf16)

    @pl.when(pl.program_id(1) == 0)
    def _():
        for buf in bufs:
            buf[0:HALO_ROWS, :] = jnp.zeros((HALO_ROWS, COL_CHUNK), f32)
        st_ref[...] = jnp.zeros(st_ref.shape, f32)

    _mix_in(h_ref, pm_ref, w_ref, mid_ref, dt_ref, perm_ref, xnat_ref, xn_ref, bufs)
    _mix_out(h_ref, mid_ref, dt_ref, sp_ref, wo_ref, o_ref,
             expand_ref, unperm_ref, st_ref, y_ref, yn_ref, mg_ref, mgn_ref)


def _mixer(h, layer, in_params, w_all, out_params, w_out):
    b, s, _ = h.shape
    ts = min(MIX_TILE, s)
    assert s % ts == 0 and ts % SSM_CHUNK == 0
    tok = pl.BlockSpec((None, ts, D_MODEL), lambda bi, j: (bi, j, 0))
    return pl.pallas_call(
        _mixer_kernel,
        grid=(b, s // ts),
        in_specs=[tok, _layer_spec(layer, PM_ROWS, SSM_CONV_DIM), _layer_spec(layer, D_MODEL, PROJ_COLS),
                  _layer_spec(layer, SP_ROWS, SSM_INNER), _layer_spec(layer, WO_ROWS, D_MODEL)],
        out_specs=tok,
        out_shape=jax.ShapeDtypeStruct((b, s, D_MODEL), f32),
        scratch_shapes=[pltpu.VMEM((ts, ts), bf16), pltpu.VMEM((ts, ts), bf16),
                        pltpu.VMEM((2 * LANES, SSM_INNER), bf16),
                        pltpu.VMEM((ts, D_MODEL), bf16), pltpu.VMEM((ts, D_MODEL), bf16),
                        pltpu.VMEM((ts, MID_COLS), bf16), pltpu.VMEM((ts, LANES), f32),
                        pltpu.VMEM((SSM_STATE, SSM_INNER), f32), pltpu.VMEM((ts, SSM_INNER), f32),
                        pltpu.VMEM((ts, SSM_INNER), bf16), pltpu.VMEM((ts, D_MODEL), bf16),
                        pltpu.VMEM((ts, D_MODEL), bf16)]
        + [pltpu.VMEM((HALO_ROWS + ts // SSM_CHUNK * CHUNK_PITCH, COL_CHUNK), f32)] * (N_SC_CHUNKS + N_M_CHUNKS),
        compiler_params=pltpu.CompilerParams(dimension_semantics=("arbitrary", "arbitrary"),
                                             vmem_limit_bytes=VMEM_LIMIT),
        name="mixer",
    )(h, in_params, w_all, out_params, w_out)


def _row(x):
    return x.reshape(x.shape[0], 1, x.shape[1])


def _rows(x, n_rows, width):
    return jnp.pad(x, ((0, 0), (0, n_rows - x.shape[1]), (0, width - x.shape[2])))


def _arrange_w_in(w_in):
    w = w_in.astype(bf16)
    cx_tiles = []
    for c0 in range(0, SC_WIDTH, CX_TILE):
        cx_tiles += [w[..., SC_WIDTH + c0:SC_WIDTH + c0 + CX_TILE],
                     w[..., 2 * SC_WIDTH + c0:2 * SC_WIDTH + c0 + CX_TILE]]
    return jnp.concatenate(
        [w[..., :SC_WIDTH], *cx_tiles, w[..., 3 * SC_WIDTH:DT_SRC], w[..., DT_SRC + SSM_HEADS:],
         w[..., DT_SRC:DT_SRC + SSM_HEADS], jnp.zeros(w.shape[:2] + (LANES - SSM_HEADS,), bf16)], axis=-1)


def kernel(x, p, ffn1_norm, ffn1_wg, ffn1_wu, ffn1_wd, mix_norm, w_in, sc_conv_w, sc_w_out, m_conv_w, m_conv_b, m_dt_bias, m_A_log, m_D, m_norm, m_w_out, w_o, ffn2_norm, ffn2_wg, ffn2_wu, ffn2_wd, ple_norm, ple_w_gate, ple_w_proj, final_norm):
    depth = w_in.shape[0]
    b, s, d = x.shape
    t = b * s
    cast = lambda w: w.astype(bf16)
    f1 = (_row(ffn1_norm), cast(ffn1_wg), cast(ffn1_wu), cast(ffn1_wd))
    f2 = (_row(ffn2_norm), cast(ffn2_wg), cast(ffn2_wu), cast(ffn2_wd))
    ple_w = (_row(ple_norm), cast(ple_w_gate), cast(ple_w_proj))
    rep = lambda w: jnp.repeat(w, ROW_BLOCK, axis=1)
    wide = lambda w, n_rows: _rows(w, n_rows, SSM_CONV_DIM)
    mix_in_p = jnp.concatenate(
        [rep(_row(m_conv_b)), rep(m_conv_w), wide(rep(sc_conv_w), SC_KERNEL * ROW_BLOCK),
         wide(_row(mix_norm), SUBLANES), wide(_row(m_dt_bias), SUBLANES)], axis=1)
    mix_out_p = jnp.concatenate(
        [_row(jnp.repeat(m_D, SSM_HEADDIM, axis=1)), _row(m_norm),
         _rows(_row(m_A_log), SP_ROWS - 2, SSM_INNER)], axis=1)
    w_all = _arrange_w_in(w_in)
    w_out = jnp.concatenate([cast(m_w_out), cast(sc_w_out), cast(w_o)], axis=1)
    p2 = p.reshape(depth, t, PLE_DIM)
    fnorm = final_norm.reshape(1, d)

    h = x.reshape(t, d)
    for i in range(depth):
        h = _ffn(h, i, *f1)
        h = _mixer(h.reshape(b, s, d), i, mix_in_p, w_all, mix_out_p, w_out).reshape(t, d)
        h = _ffn(h, i, *f2, ple_args=(p2,) + ple_w, final_norm=fnorm)
    return h.reshape(b, s, d)
```

```python
import functools

import jax
import jax.numpy as jnp
from jax import lax
from jax.experimental import pallas as pl
from jax.experimental.pallas import tpu as pltpu

D_MODEL = 1024
D_FF = 2816
PLE_DIM = 256
SC_WIDTH = 1024
SC_KERNEL = 3
SSM_INNER = 2048
SSM_HEADDIM = 64
SSM_HEADS = 32
SSM_GROUPS = 4
SSM_STATE = 128
SSM_CONV = 4
SSM_CHUNK = 128
SSM_CONV_DIM = SSM_INNER + 2 * SSM_GROUPS * SSM_STATE
GROUP_WIDTH = SSM_INNER // SSM_GROUPS
HEADS_PER_GROUP = SSM_HEADS // SSM_GROUPS
EPS = 1e-6
LOG2E = 1.4426950408889634

LANES = 128
SUBLANES = 8
ROW_BLOCK = 16
COL_CHUNK = 512

CX_TILE = COL_CHUNK // 2
COL_SCB = 0
COL_CX = SC_WIDTH
COL_Z = 3 * SC_WIDTH
COL_XBC = COL_Z + SSM_INNER
COL_GATE = COL_XBC + SSM_CONV_DIM
COL_DT = COL_GATE + 2 * D_MODEL
PROJ_COLS = COL_DT + LANES
DT_SRC = COL_GATE

MID_SCP = 0
MID_Z = MID_SCP + SC_WIDTH
MID_XS = MID_Z + SSM_INNER
MID_BC = MID_XS + SSM_INNER
MID_GT = MID_BC + 2 * SSM_GROUPS * SSM_STATE
MID_COLS = MID_GT + 2 * D_MODEL

PM_MCB = 0
PM_MCW = PM_MCB + ROW_BLOCK
PM_SCW = PM_MCW + SSM_CONV * ROW_BLOCK
PM_NORM = PM_SCW + SC_KERNEL * ROW_BLOCK
PM_DTB = PM_NORM + SUBLANES
PM_ROWS = PM_DTB + SUBLANES

SP_D, SP_NORM, SP_ALOG, SP_ROWS = 0, 1, 2, SUBLANES
WO_M, WO_SC, WO_O = 0, SSM_INNER, SSM_INNER + SC_WIDTH
WO_ROWS = WO_O + D_MODEL

FF_SPLITS = tuple((c, min(c + 512, D_FF)) for c in range(0, D_FF, 512))

FFN_TILE = 1024
MIX_TILE = 256
VMEM_LIMIT = 56 * 1024 * 1024

bf16 = jnp.bfloat16
f32 = jnp.float32


def _rms(x, g):
    return x * lax.rsqrt(jnp.mean(x * x, axis=-1, keepdims=True) + EPS) * g


def _sigmoid(x):
    return 0.5 * jnp.tanh(0.5 * x) + 0.5


def _silu(x):
    return x * _sigmoid(x)


def _dot(a, b):
    return jnp.dot(a, b, preferred_element_type=f32)


def _split2(x):
    hi = x.astype(bf16)
    lo = (x - hi.astype(f32)).astype(bf16)
    return hi, lo


def _split3(x):
    hi = x.astype(bf16)
    r = x - hi.astype(f32)
    mid = r.astype(bf16)
    lo = (r - mid.astype(f32)).astype(bf16)
    return hi, mid, lo


def _ffn_kernel(*refs, ple, final):
    if ple:
        (h_ref, g_ref, wg_ref, wu_ref, wd_ref, p_ref, pg_ref, pwg_ref, pwp_ref, fg_ref, o_ref, act_ref) = refs
    else:
        (h_ref, g_ref, wg_ref, wu_ref, wd_ref, o_ref, act_ref) = refs
    h = h_ref[...]
    xn = _rms(h, g_ref[...]).astype(bf16)
    for lo, hi in FF_SPLITS:
        gate = _dot(xn, wg_ref[:, lo:hi])
        up = _dot(xn, wu_ref[:, lo:hi])
        act_ref[:, lo:hi] = (_silu(gate) * up).astype(bf16)
    h = h + 0.5 * _dot(act_ref[...], wd_ref[...])
    if ple:
        gate = _sigmoid(_dot(_rms(h, pg_ref[...]).astype(bf16), pwg_ref[...]))
        h = h + gate * _dot(p_ref[...].astype(bf16), pwp_ref[...])
    if final:
        h = _rms(h, fg_ref[...])
    o_ref[...] = h


def _layer_spec(layer, rows, cols):
    return pl.BlockSpec((None, rows, cols), lambda *_: (layer, 0, 0), pipeline_mode=pl.Buffered(1))


def _ffn(h, layer, norm, wg, wu, wd, ple_args=None, final_norm=None):
    t = h.shape[0]
    tm = min(FFN_TILE, t)
    assert t % tm == 0
    tok = lambda cols: pl.BlockSpec((tm, cols), lambda i: (i, 0))
    in_specs = [tok(D_MODEL), _layer_spec(layer, 1, D_MODEL), _layer_spec(layer, D_MODEL, D_FF),
                _layer_spec(layer, D_MODEL, D_FF), _layer_spec(layer, D_FF, D_MODEL)]
    args = [h, norm, wg, wu, wd]
    ple = ple_args is not None
    if ple:
        p, pnorm, pwg, pwp = ple_args
        in_specs += [pl.BlockSpec((None, tm, PLE_DIM), lambda i: (layer, i, 0)),
                     _layer_spec(layer, 1, D_MODEL), _layer_spec(layer, D_MODEL, D_MODEL),
                     _layer_spec(layer, PLE_DIM, D_MODEL),
                     pl.BlockSpec((1, D_MODEL), lambda i: (0, 0))]
        args += [p, pnorm, pwg, pwp, final_norm]
    return pl.pallas_call(
        functools.partial(_ffn_kernel, ple=ple, final=ple and layer == norm.shape[0] - 1),
        grid=(t // tm,),
        in_specs=in_specs,
        out_specs=tok(D_MODEL),
        out_shape=jax.ShapeDtypeStruct((t, D_MODEL), f32),
        scratch_shapes=[pltpu.VMEM((tm, D_FF), bf16)],
        compiler_params=pltpu.CompilerParams(dimension_semantics=("arbitrary",),
                                             vmem_limit_bytes=VMEM_LIMIT),
        name=f"ffn_ple{int(ple)}",
    )(*args)


TILES_PER_CHUNK = SSM_CHUNK // SUBLANES
HALO_TILES = SSM_CONV - 1
HALO_ROWS = HALO_TILES * SUBLANES
CHUNK_PITCH = HALO_ROWS + SSM_CHUNK
N_SC_CHUNKS = SC_WIDTH // COL_CHUNK
N_M_CHUNKS = SSM_CONV_DIM // COL_CHUNK


def _time_of_row(q):
    in_chunk = q & (SSM_CHUNK - 1)
    return (q - in_chunk) + (in_chunk & (SUBLANES - 1)) * TILES_PER_CHUNK + lax.shift_right_logical(
        in_chunk, SUBLANES.bit_length() - 1)


def _time_permutation(ts, inverse=False):
    rows = lax.broadcasted_iota(jnp.int32, (ts, ts), 0)
    cols = lax.broadcasted_iota(jnp.int32, (ts, ts), 1)
    hit = (rows == _time_of_row(cols)) if inverse else (_time_of_row(rows) == cols)
    return hit.astype(bf16)


def _chunk_data(c):
    return HALO_ROWS + c * CHUNK_PITCH + HALO_ROWS


def _stage(buf_ref, x):
    n_chunks = x.shape[0] // SSM_CHUNK
    tail = SSM_CHUNK - HALO_ROWS
    first_sublane = lax.broadcasted_iota(jnp.int32, (SUBLANES, x.shape[1]), 0) == 0
    for c in range(n_chunks):
        buf_ref[_chunk_data(c):_chunk_data(c) + SSM_CHUNK, :] = x[c * SSM_CHUNK:(c + 1) * SSM_CHUNK, :]
    for c in range(n_chunks):
        prev0 = 0 if c == 0 else _chunk_data(c - 1) + tail
        cur0 = _chunk_data(c) + tail
        halo0 = _chunk_data(c) - HALO_ROWS
        for m in range(HALO_TILES):
            tile = slice(m * SUBLANES, (m + 1) * SUBLANES)
            cur = buf_ref[cur0 + tile.start:cur0 + tile.stop, :]
            prev = buf_ref[prev0 + tile.start:prev0 + tile.stop, :]
            buf_ref[halo0 + tile.start:halo0 + tile.stop, :] = jnp.where(
                first_sublane, pltpu.roll(prev, 1, axis=0), pltpu.roll(cur, 1, axis=0))


def _carry(buf_ref, ts):
    last = _chunk_data(ts // SSM_CHUNK - 1) + SSM_CHUNK - HALO_ROWS
    buf_ref[0:HALO_ROWS, :] = buf_ref[last:last + HALO_ROWS, :]


def _conv_block(buf_ref, w_ref, w0, taps, r, wcols):
    base = _chunk_data(r // SSM_CHUNK) + r % SSM_CHUNK
    out = None
    for k in range(taps):
        off = base - (taps - 1 - k) * SUBLANES
        term = buf_ref[off:off + ROW_BLOCK, :] * w_ref[w0 + k * ROW_BLOCK:w0 + (k + 1) * ROW_BLOCK, wcols]
        out = term if out is None else out + term
    return out


def _mix_in(h_ref, pm_ref, w_ref, mid_ref, dt_ref, perm_ref, xnat_ref, xn_ref, bufs):
    scbufs, mbufs = bufs[:N_SC_CHUNKS], bufs[N_SC_CHUNKS:]
    ts = h_ref.shape[0]
    row_blocks = range(0, ts, ROW_BLOCK)

    gain = pm_ref[PM_NORM:PM_NORM + 1, 0:D_MODEL]
    for r in row_blocks:
        rows = slice(r, r + ROW_BLOCK)
        xnat_ref[rows, :] = _rms(h_ref[rows, :], gain).astype(bf16)
    xn_ref[...] = _dot(perm_ref[...], xnat_ref[...]).astype(bf16)

    def proj(c0, width=COL_CHUNK):
        return _dot(xn_ref[...], w_ref[:, c0:c0 + width])

    for i in range(N_SC_CHUNKS):
        prods = []
        for c0 in range(2 * i * COL_CHUNK, 2 * (i + 1) * COL_CHUNK, COL_CHUNK):
            cx = proj(COL_CX + c0)
            prods.append(cx[:, :CX_TILE] * cx[:, CX_TILE:])
        _stage(scbufs[i], jnp.concatenate(prods, axis=1))

    def sc_job(i):
        cols = slice(i * COL_CHUNK, (i + 1) * COL_CHUNK)
        scb = proj(COL_SCB + i * COL_CHUNK)
        for r in row_blocks:
            conv = _conv_block(scbufs[i], pm_ref, PM_SCW, SC_KERNEL, r, cols)
            mid_ref[r:r + ROW_BLOCK, MID_SCP + cols.start:MID_SCP + cols.stop] = (
                scb[r:r + ROW_BLOCK, :] * conv).astype(bf16)
        _carry(scbufs[i], ts)

    def m_start(i):
        _stage(mbufs[i], proj(COL_XBC + i * COL_CHUNK))

    def m_finish(i):
        cols = slice(i * COL_CHUNK, (i + 1) * COL_CHUNK)
        for r in row_blocks:
            conv = (_conv_block(mbufs[i], pm_ref, PM_MCW, SSM_CONV, r, cols)
                    + pm_ref[PM_MCB:PM_MCB + ROW_BLOCK, cols])
            mid_ref[r:r + ROW_BLOCK, MID_XS + cols.start:MID_XS + cols.stop] = _silu(conv).astype(bf16)
        _carry(mbufs[i], ts)

    def z_job(i):
        c0 = i * COL_CHUNK
        mid_ref[:, MID_Z + c0:MID_Z + c0 + COL_CHUNK] = proj(COL_Z + c0).astype(bf16)

    def gate_job(i):
        c0 = i * COL_CHUNK
        mid_ref[:, MID_GT + c0:MID_GT + c0 + COL_CHUNK] = _sigmoid(proj(COL_GATE + c0)).astype(bf16)

    light = ([(z_job, i) for i in range(SSM_INNER // COL_CHUNK)]
             + [(gate_job, i) for i in range(2 * D_MODEL // COL_CHUNK)])
    for i in range(N_SC_CHUNKS):
        sc_job(i)
        job, j = light.pop(0)
        job(j)
    m_start(0)
    for i in range(N_M_CHUNKS):
        if i + 1 < N_M_CHUNKS:
            m_start(i + 1)
        job, j = light.pop(0)
        job(j)
        m_finish(i)
    assert not light
    dt_ref[...] = jax.nn.softplus(proj(COL_DT, LANES) + pm_ref[PM_DTB:PM_DTB + 1, 0:LANES])


def _chunk_time(q):
    return (q & (SUBLANES - 1)) * TILES_PER_CHUNK + lax.shift_right_logical(q, SUBLANES.bit_length() - 1)


def _ssd_chunk(rows, mid_ref, dt_ref, a_row, d_row, expand_ref, st_ref, y_ref):
    L = SSM_CHUNK
    nbc = SSM_GROUPS * SSM_STATE
    dt = dt_ref[rows, :]
    a = dt * a_row
    causal = (_chunk_time(lax.broadcasted_iota(jnp.int32, (L, L), 0))
              >= _chunk_time(lax.broadcasted_iota(jnp.int32, (L, L), 1)))
    tril = causal.astype(bf16)
    acum = _dot(jnp.concatenate([tril] * 3, axis=1), jnp.concatenate(_split3(a), axis=0))
    acum2 = acum * LOG2E
    acum2_t = acum2.T
    dt_t = dt.T
    a_last = acum[L - 1:L, :]
    w = jnp.exp(a_last - acum) * dt
    stacked = jnp.concatenate([w, jnp.exp(acum)], axis=0)
    expd = _dot(jnp.concatenate(_split2(stacked), axis=1), expand_ref[...])
    lane = lax.broadcasted_iota(jnp.int32, (L, LANES), 1)
    low_half = lane < SSM_HEADDIM
    for g in range(SSM_GROUPS):
        cs = slice(g * GROUP_WIDTH, (g + 1) * GROUP_WIDTH)
        bg = mid_ref[rows, MID_BC + g * SSM_STATE:MID_BC + (g + 1) * SSM_STATE]
        cg = mid_ref[rows, MID_BC + nbc + g * SSM_STATE:MID_BC + nbc + (g + 1) * SSM_STATE]
        st = st_ref[:, cs]
        y_off = _dot(cg, st.astype(bf16)) * expd[L:, cs]
        xw = (mid_ref[rows, MID_XS + cs.start:MID_XS + cs.stop].astype(f32) * expd[:L, cs]).astype(bf16)
        new = lax.dot_general(bg, xw, (((0,), (0,)), ((), ())), preferred_element_type=f32)
        st_ref[:, cs] = st * expd[2 * L - 1:2 * L, cs] + new
        cb = lax.dot_general(cg, bg, (((1,), (1,)), ((), ())), preferred_element_type=f32)
        for pr in range(HEADS_PER_GROUP // 2):
            ms = []
            for k in range(2):
                hd = g * HEADS_PER_GROUP + 2 * pr + k
                diff = acum2[:, hd:hd + 1] - acum2_t[hd:hd + 1, :]
                decay = jnp.exp2(jnp.where(causal, diff, -jnp.inf))
                ms.append((cb * decay * dt_t[hd:hd + 1, :]).astype(bf16))
            pc = slice(g * GROUP_WIDTH + pr * LANES, g * GROUP_WIDTH + (pr + 1) * LANES)
            xp = mid_ref[rows, MID_XS + pc.start:MID_XS + pc.stop]
            zero = jnp.zeros_like(xp)
            rhs = jnp.concatenate([jnp.where(low_half, xp, zero), jnp.where(low_half, zero, xp)], axis=0)
            y_ref[rows, pc] = (_dot(jnp.concatenate(ms, axis=1), rhs)
                               + y_off[:, pr * LANES:(pr + 1) * LANES] + d_row[:, pc] * xp.astype(f32))


def _mix_out(h_ref, mid_ref, dt_ref, sp_ref, w_ref, o_ref,
             expand_ref, unperm_ref, st_ref, y_ref, yn_ref, mg_ref, mgn_ref):
    ts = h_ref.shape[0]
    a_row = -jnp.exp(sp_ref[SP_ALOG:SP_ALOG + 1, 0:LANES])
    d_row = sp_ref[SP_D:SP_D + 1, :]
    for c in range(ts // SSM_CHUNK):
        _ssd_chunk(slice(c * SSM_CHUNK, (c + 1) * SSM_CHUNK), mid_ref, dt_ref, a_row, d_row,
                   expand_ref, st_ref, y_ref)

    for g in range(SSM_GROUPS):
        cs = slice(g * GROUP_WIDTH, (g + 1) * GROUP_WIDTH)
        for r in range(0, ts, ROW_BLOCK):
            rows = slice(r, r + ROW_BLOCK)
            yz = y_ref[rows, cs] * _silu(mid_ref[rows, MID_Z + cs.start:MID_Z + cs.stop].astype(f32))
            yn_ref[rows, cs] = _rms(yz, sp_ref[SP_NORM:SP_NORM + 1, cs]).astype(bf16)
    for c0 in range(0, D_MODEL, COL_CHUNK):
        cols = slice(c0, c0 + COL_CHUNK)
        y_a = _dot(mid_ref[:, MID_SCP:MID_SCP + SC_WIDTH], w_ref[WO_SC:WO_SC + SC_WIDTH, cols])
        y_m = _dot(yn_ref[...], w_ref[WO_M:WO_M + SSM_INNER, cols])
        gate_a = mid_ref[:, MID_GT + c0:MID_GT + c0 + COL_CHUNK].astype(f32)
        gate_m = mid_ref[:, MID_GT + D_MODEL + c0:MID_GT + D_MODEL + c0 + COL_CHUNK].astype(f32)
        mg_ref[:, cols] = (gate_a * y_a + gate_m * y_m).astype(bf16)
    mgn_ref[...] = _dot(unperm_ref[...], mg_ref[...]).astype(bf16)
    for c0 in range(0, D_MODEL, COL_CHUNK):
        cols = slice(c0, c0 + COL_CHUNK)
        o_ref[:, cols] = h_ref[:, cols] + _dot(mgn_ref[...], w_ref[WO_O:WO_O + D_MODEL, cols])


def _mixer_kernel(h_ref, pm_ref, w_ref, sp_ref, wo_ref, o_ref,
                  perm_ref, unperm_ref, expand_ref, xnat_ref, xn_ref, mid_ref, dt_ref,
                  st_ref, y_ref, yn_ref, mg_ref, mgn_ref, *bufs):
    ts = h_ref.shape[0]

    @pl.when((pl.program_id(0) == 0) & (pl.program_id(1) == 0))
    def _():
        perm_ref[...] = _time_permutation(ts)
        unperm_ref[...] = _time_permutation(ts, inverse=True)
        head_of_row = lax.broadcasted_iota(jnp.int32, expand_ref.shape, 0) & (LANES - 1)
        head_of_channel = lax.shift_right_logical(lax.broadcasted_iota(jnp.int32, expand_ref.shape, 1),
                                                  SSM_HEADDIM.bit_length() - 1)
        expand_ref[...] = (head_of_row == head_of_channel).astype(bf16)

    @pl.when(pl.program_id(1) == 0)
    def _():
        for buf in bufs:
            buf[0:HALO_ROWS, :] = jnp.zeros((HALO_ROWS, COL_CHUNK), f32)
        st_ref[...] = jnp.zeros(st_ref.shape, f32)

    _mix_in(h_ref, pm_ref, w_ref, mid_ref, dt_ref, perm_ref, xnat_ref, xn_ref, bufs)
    _mix_out(h_ref, mid_ref, dt_ref, sp_ref, wo_ref, o_ref,
             expand_ref, unperm_ref, st_ref, y_ref, yn_ref, mg_ref, mgn_ref)


def _mixer(h, layer, in_params, w_all, out_params, w_out):
    b, s, _ = h.shape
    ts = min(MIX_TILE, s)
    assert s % ts == 0 and ts % SSM_CHUNK == 0
    tok = pl.BlockSpec((None, ts, D_MODEL), lambda bi, j: (bi, j, 0))
    return pl.pallas_call(
        _mixer_kernel,
        grid=(b, s // ts),
        in_specs=[tok, _layer_spec(layer, PM_ROWS, SSM_CONV_DIM), _layer_spec(layer, D_MODEL, PROJ_COLS),
                  _layer_spec(layer, SP_ROWS, SSM_INNER), _layer_spec(layer, WO_ROWS, D_MODEL)],
        out_specs=tok,
        out_shape=jax.ShapeDtypeStruct((b, s, D_MODEL), f32),
        scratch_shapes=[pltpu.VMEM((ts, ts), bf16), pltpu.VMEM((ts, ts), bf16),
                        pltpu.VMEM((2 * LANES, SSM_INNER), bf16),
                        pltpu.VMEM((ts, D_MODEL), bf16), pltpu.VMEM((ts, D_MODEL), bf16),
                        pltpu.VMEM((ts, MID_COLS), bf16), pltpu.VMEM((ts, LANES), f32),
                        pltpu.VMEM((SSM_STATE, SSM_INNER), f32), pltpu.VMEM((ts, SSM_INNER), f32),
                        pltpu.VMEM((ts, SSM_INNER), bf16), pltpu.VMEM((ts, D_MODEL), bf16),
                        pltpu.VMEM((ts, D_MODEL), bf16)]
        + [pltpu.VMEM((HALO_ROWS + ts // SSM_CHUNK * CHUNK_PITCH, COL_CHUNK), f32)] * (N_SC_CHUNKS + N_M_CHUNKS),
        compiler_params=pltpu.CompilerParams(dimension_semantics=("arbitrary", "arbitrary"),
                                             vmem_limit_bytes=VMEM_LIMIT),
        name="mixer",
    )(h, in_params, w_all, out_params, w_out)


def _row(x):
    return x.reshape(x.shape[0], 1, x.shape[1])


def _rows(x, n_rows, width):
    return jnp.pad(x, ((0, 0), (0, n_rows - x.shape[1]), (0, width - x.shape[2])))


def _arrange_w_in(w_in):
    w = w_in.astype(bf16)
    cx_tiles = []
    for c0 in range(0, SC_WIDTH, CX_TILE):
        cx_tiles += [w[..., SC_WIDTH + c0:SC_WIDTH + c0 + CX_TILE],
                     w[..., 2 * SC_WIDTH + c0:2 * SC_WIDTH + c0 + CX_TILE]]
    return jnp.concatenate(
        [w[..., :SC_WIDTH], *cx_tiles, w[..., 3 * SC_WIDTH:DT_SRC], w[..., DT_SRC + SSM_HEADS:],
         w[..., DT_SRC:DT_SRC + SSM_HEADS], jnp.zeros(w.shape[:2] + (LANES - SSM_HEADS,), bf16)], axis=-1)


def kernel(x, p, ffn1_norm, ffn1_wg, ffn1_wu, ffn1_wd, mix_norm, w_in, sc_conv_w, sc_w_out, m_conv_w, m_conv_b, m_dt_bias, m_A_log, m_D, m_norm, m_w_out, w_o, ffn2_norm, ffn2_wg, ffn2_wu, ffn2_wd, ple_norm, ple_w_gate, ple_w_proj, final_norm):
    depth = w_in.shape[0]
    b, s, d = x.shape
    t = b * s
    cast = lambda w: w.astype(bf16)
    f1 = (_row(ffn1_norm), cast(ffn1_wg), cast(ffn1_wu), cast(ffn1_wd))
    f2 = (_row(ffn2_norm), cast(ffn2_wg), cast(ffn2_wu), cast(ffn2_wd))
    ple_w = (_row(ple_norm), cast(ple_w_gate), cast(ple_w_proj))
    rep = lambda w: jnp.repeat(w, ROW_BLOCK, axis=1)
    wide = lambda w, n_rows: _rows(w, n_rows, SSM_CONV_DIM)
    mix_in_p = jnp.concatenate(
        [rep(_row(m_conv_b)), rep(m_conv_w), wide(rep(sc_conv_w), SC_KERNEL * ROW_BLOCK),
         wide(_row(mix_norm), SUBLANES), wide(_row(m_dt_bias), SUBLANES)], axis=1)
    mix_out_p = jnp.concatenate(
        [_row(jnp.repeat(m_D, SSM_HEADDIM, axis=1)), _row(m_norm),
         _rows(_row(m_A_log), SP_ROWS - 2, SSM_INNER)], axis=1)
    w_all = _arrange_w_in(w_in)
    w_out = jnp.concatenate([cast(m_w_out), cast(sc_w_out), cast(w_o)], axis=1)
    p2 = p.reshape(depth, t, PLE_DIM)
    fnorm = final_norm.reshape(1, d)

    h = x.reshape(t, d)
    for i in range(depth):
        h = _ffn(h, i, *f1)
        h = _mixer(h.reshape(b, s, d), i, mix_in_p, w_all, mix_out_p, w_out).reshape(t, d)
        h = _ffn(h, i, *f2, ple_args=(p2,) + ple_w, final_norm=fnorm)
    return h.reshape(b, s, d)
```
